```python
import math
import jax, jax.numpy as jnp
from jax import lax
import numpy as np

D_MODEL = 1024
BATCH = 4
SEQ = 4096
DEPTH = 4
DEC_BATCH = 128
DEC_SEQ = 8
PAST_LEN = 2048
PAGE_SIZE = 128

N_META = 16
BLOCK = 128
N_EVEN = (DEPTH + 1) // 2
N_ODD = DEPTH // 2
H_A = 8
DH_A = 64
W_A = H_A * DH_A
SB_BIAS_LO = 5.0
SB_BIAS_HI = 9.0
H_B = 8
DH_B = 64
W_B = H_B * DH_B
R_DECAY = 64
R_ICLR = 64
R_GATE = 128
B_COLS = 3 * W_B + R_DECAY + R_ICLR + R_GATE
AB_COLS = 3 * W_A + B_COLS
H_C = 4
DK_C = 128
DV_C = 256
QK_C = H_C * DK_C
V_C = H_C * DV_C
CONV_W = 4
C_COLS = 2 * QK_C + V_C + 2 * H_C + V_C
D_FF = math.ceil(8 * D_MODEL / 3 / 256) * 256
RMS_EPS = 1e-6
GN_EPS = 64e-5

kernel_name = 'sb_rwkv7_mlstm_hybrid_step'


def rms_norm(x, g):
    xf = x.astype(jnp.float32)
    y = xf * lax.rsqrt(jnp.mean(xf * xf, axis=-1, keepdims=True) + RMS_EPS)
    return (y * g.astype(jnp.float32)).astype(x.dtype)


def head_layer_norm(y, eps):
    yf = y.astype(jnp.float32)
    mu = jnp.mean(yf, axis=-1, keepdims=True)
    var = jnp.mean(jnp.square(yf - mu), axis=-1, keepdims=True)
    return (yf - mu) * lax.rsqrt(var + eps)


def swiglu(h, w_gate, w_up, w_down):
    return (jax.nn.silu(h @ w_gate) * (h @ w_up)) @ w_down


def stick_breaking(q, k, v, q_pos, k_pos, bias):
    z = (jnp.einsum('bqhd,bkhd->bhqk', q, k).astype(jnp.float32) * (DH_A ** -0.5)
         + bias.astype(jnp.float32)[None, :, None, None])
    mask = k_pos[None, :] < q_pos[:, None]
    log_beta = jax.nn.log_sigmoid(z)
    log_keep = jnp.where(mask, jax.nn.log_sigmoid(-z), 0.0)
    later = lax.cumsum(log_keep, axis=3, reverse=True) - log_keep
    att = jnp.where(mask, jnp.exp(log_beta + later), 0.0)
    return jnp.einsum('bhqk,bkhd->bqhd', att.astype(v.dtype), v)


def sb_prompt(q, k, v, bias):
    b, t, h, d = q.shape
    pos = jnp.arange(t)
    out_meta = stick_breaking(q[:, :N_META], k, v, pos[:N_META], pos, bias)
    nb = (t - N_META) // BLOCK
    qb = jnp.swapaxes(q[:, N_META:].reshape(b, nb, BLOCK, h, d), 0, 1)
    pb = pos[N_META:].reshape(nb, BLOCK)
    out = lax.map(lambda blk: stick_breaking(blk[0], k, v, blk[1], pos, bias), (qb, pb))
    out = jnp.swapaxes(out, 0, 1).reshape(b, nb * BLOCK, h, d)
    return jnp.concatenate([out_meta, out], axis=1)


def sb_sample(q, k, v, pool_k, pool_v, page_table, bias):
    db, n_pages = page_table.shape
    past = n_pages * PAGE_SIZE
    pk = pool_k[page_table].reshape(db, past, H_A, DH_A).astype(k.dtype)
    pv = pool_v[page_table].reshape(db, past, H_A, DH_A).astype(v.dtype)
    k_all = jnp.concatenate([pk, k], axis=1)
    v_all = jnp.concatenate([pv, v], axis=1)
    t = q.shape[1]
    q_pos = past + jnp.arange(t)
    k_pos = jnp.arange(past + t)
    return stick_breaking(q, k_all, v_all, q_pos, k_pos, bias)


def rwkv_branch(braw, shift_prev, wkv0, mu, w0, w_decay_up, a0, w_iclr_up, w_gate_up, k_k, k_a, r_k, ln_w, ln_b):
    b, t, _ = braw.shape
    f32 = jnp.float32
    prev = jnp.concatenate([shift_prev[:, None].astype(braw.dtype), braw[:, :-1]], axis=1)
    u = braw + (prev - braw) * mu
    r, k, v, ud, ua, ug = jnp.split(u, [W_B, 2 * W_B, 3 * W_B, 3 * W_B + R_DECAY, 3 * W_B + R_DECAY + R_ICLR], axis=-1)
    w_log = -jax.nn.softplus(-(w0 + jnp.tanh(ud) @ w_decay_up).astype(f32)) - 0.5
    decay = jnp.exp(-jnp.exp(w_log))
    a = jax.nn.sigmoid((a0 + ua @ w_iclr_up).astype(f32))
    g = jax.nn.sigmoid(ug) @ w_gate_up

    def heads(z):
        return z.astype(f32).reshape(b, t, H_B, DH_B)

    kk = heads(k * k_k)
    kk = kk / jnp.maximum(jnp.linalg.norm(kk, axis=-1, keepdims=True), 1e-12)
    k_mod = k.astype(f32) * (1.0 + (a - 1.0) * k_a)
    r_h, k_h, v_h, a_h, w_h = heads(r), heads(k_mod), heads(v), heads(a), heads(decay)

    def step(state, inp):
        r_t, w_t, k_t, v_t, kk_t, a_t = inp
        sa = jnp.einsum('bhvk,bhk->bhv', state, kk_t)
        state = (state * w_t[:, :, None, :]
                 - sa[..., None] * (kk_t * a_t)[:, :, None, :]
                 + v_t[..., None] * k_t[:, :, None, :])
        return state, jnp.einsum('bhvk,bhk->bhv', state, r_t)

    xs = tuple(jnp.swapaxes(z, 0, 1) for z in (r_h, w_h, k_h, v_h, kk, a_h))
    wkv, y = lax.scan(step, wkv0.astype(f32), xs)
    y = jnp.swapaxes(y, 0, 1)
    yn = head_layer_norm(y, GN_EPS).reshape(b, t, W_B) * ln_w + ln_b
    bonus = jnp.sum(r_h * k_h * r_k.reshape(H_B, DH_B), axis=-1, keepdims=True) * v_h
    out = (yn + bonus.reshape(b, t, W_B)) * g
    return out.astype(braw.dtype), wkv, braw[:, -1]


def ab_mixer(h, shift_prev, wkv0, attend_fn, w_in, w_out, rw):
    b, t, _ = h.shape
    proj = h @ w_in
    q, k, v, braw = jnp.split(proj, [W_A, 2 * W_A, 3 * W_A], axis=-1)
    q = q.reshape(b, t, H_A, DH_A)
    k = k.reshape(b, t, H_A, DH_A)
    v = v.reshape(b, t, H_A, DH_A)
    o_a = attend_fn(q, k, v).reshape(b, t, W_A)
    o_b, wkv, shift_last = rwkv_branch(braw, shift_prev, wkv0, *rw)
    out = jnp.concatenate([o_a, o_b.astype(o_a.dtype)], axis=-1) @ w_out
    return out, k, v, wkv, shift_last


def mlstm_chunk(carry, inp):
    c, n, m = carry
    q, k, v, li, lf = inp
    L = q.shape[1]
    bcum = jnp.cumsum(lf, axis=1)
    causal = jnp.tril(jnp.ones((L, L), dtype=bool))
    d = bcum[:, :, None, :] - bcum[:, None, :, :] + li[:, None, :, :]
    d = jnp.where(causal[None, :, :, None], d, -jnp.inf)
    inter = bcum + m[:, None, :]
    m_row = jnp.maximum(inter, jnp.max(d, axis=2))
    w_intra = jnp.exp(d - m_row[:, :, None, :])
    w_inter = jnp.exp(inter - m_row)
    s = jnp.einsum('bthd,bshd->btsh', q, k) * w_intra
    num = jnp.einsum('btsh,bshv->bthv', s, v) + w_inter[..., None] * jnp.einsum('bthd,bhdv->bthv', q, c)
    den = jnp.sum(s, axis=2) + w_inter * jnp.einsum('bthd,bhd->bth', q, n)
    h = num / jnp.maximum(jnp.abs(den), jnp.exp(-m_row))[..., None]
    m_new = m_row[:, -1]
    w_end = jnp.exp(bcum[:, -1:, :] - bcum + li - m_new[:, None, :])
    dec = jnp.exp(bcum[:, -1] + m - m_new)
    c_new = dec[..., None, None] * c + jnp.einsum('bsh,bshd,bshv->bhdv', w_end, k, v)
    n_new = dec[..., None] * n + jnp.einsum('bsh,bshd->bhd', w_end, k)
    return (c_new, n_new, m_new), h


def mlstm_mixer(h, conv_buf, c0, n0, m0, w_in, conv_w, conv_b, b_i, b_f, norm_w, w_out, prompt):
    b, t, _ = h.shape
    f32 = jnp.float32
    proj = h @ w_in
    qk_raw, v, ig, fg, og = jnp.split(proj, [2 * QK_C, 2 * QK_C + V_C, 2 * QK_C + V_C + H_C, 2 * QK_C + V_C + 2 * H_C], axis=-1)
    xp = jnp.concatenate([conv_buf.astype(qk_raw.dtype), qk_raw], axis=1)
    conv = conv_b + xp[:, 0:t] * conv_w[0]
    for j in range(1, CONV_W):
        conv = conv + xp[:, j:j + t] * conv_w[j]
    q, k = jnp.split(jax.nn.silu(conv).astype(f32), 2, axis=-1)
    q = q.reshape(b, t, H_C, DK_C)
    k = k.reshape(b, t, H_C, DK_C) * (DK_C ** -0.5)
    v = v.astype(f32).reshape(b, t, H_C, DV_C)
    log_i = (ig + b_i).astype(f32)
    log_f = jax.nn.log_sigmoid((fg + b_f).astype(f32))
    carry = (c0.astype(f32), n0.astype(f32), m0.astype(f32))
    seqs = (q, k, v, log_i, log_f)
    if prompt:
        carry, h_meta = mlstm_chunk(carry, tuple(z[:, :N_META] for z in seqs))
        nb = (t - N_META) // BLOCK

        def blocks(z):
            return jnp.swapaxes(z[:, N_META:].reshape((b, nb, BLOCK) + z.shape[2:]), 0, 1)

        carry, h_rest = lax.scan(mlstm_chunk, carry, tuple(blocks(z) for z in seqs))
        h_rest = jnp.swapaxes(h_rest, 0, 1).reshape(b, nb * BLOCK, H_C, DV_C)
        hh = jnp.concatenate([h_meta, h_rest], axis=1)
    else:
        carry, hh = mlstm_chunk(carry, seqs)
    hn = head_layer_norm(hh, RMS_EPS).reshape(b, t, V_C) * norm_w
    out = (hn * jax.nn.sigmoid(og.astype(f32))).astype(h.dtype) @ w_out
    c, n, m = carry
    return out, c, n, m, xp[:, -(CONV_W - 1):]


def run_trunk(x, prompt, wkv0, shift0, c0, n0, m0, conv0, attend, P):
    ks, vs, wkvs, shifts, cs, ns, ms, convs = [], [], [], [], [], [], [], []
    for layer in range(DEPTH):
        i = layer // 2
        hn = rms_norm(x, P['g_mix_pre'][layer])
        if layer % 2 == 0:
            rw = (P['rwkv_mu'][i], P['rwkv_w0'][i], P['rwkv_w_decay_up'][i], P['rwkv_a0'][i],
                  P['rwkv_w_iclr_up'][i], P['rwkv_w_gate_up'][i], P['rwkv_k_k'][i], P['rwkv_k_a'][i],
                  P['rwkv_r_k'][i], P['rwkv_ln_w'][i], P['rwkv_ln_b'][i])
            attend_i = (lambda q, k, v, i=i: attend(i, q, k, v))
            mix, k, v, wkv, sh = ab_mixer(hn, shift0[i], wkv0[i], attend_i, P['w_in_ab'][i], P['w_out_ab'][i], rw)
            ks.append(k)
            vs.append(v)
            wkvs.append(wkv)
            shifts.append(sh)
        else:
            mix, c, n, m, cb = mlstm_mixer(hn, conv0[i], c0[i], n0[i], m0[i], P['w_in_c'][i], P['conv_w'][i],
                                           P['conv_b'][i], P['mlstm_b_i'][i], P['mlstm_b_f'][i],
                                           P['mlstm_norm_w'][i], P['w_out_c'][i], prompt)
            cs.append(c)
            ns.append(n)
            ms.append(m)
            convs.append(cb)
        x = x + rms_norm(mix.astype(x.dtype), P['g_mix_post'][layer])
        hn = rms_norm(x, P['g_ffn_pre'][layer])
        ff = swiglu(hn, P['w_ffn_gate'][layer], P['w_ffn_up'][layer], P['w_ffn_down'][layer])
        x = x + rms_norm(ff.astype(x.dtype), P['g_ffn_post'][layer])
    return (x, jnp.stack(ks), jnp.stack(vs), jnp.stack(wkvs), jnp.stack(shifts),
            jnp.stack(cs), jnp.stack(ns), jnp.stack(ms), jnp.stack(convs))


def setup_inputs(seed: int = 0) -> dict:
    key = jax.random.key(seed)
    ks = jax.random.split(key, 41)
    f32 = jnp.float32

    def nrm(i, shape, scale):
        return scale * jax.random.normal(ks[i], shape, f32)

    n_pages = PAST_LEN // PAGE_SIZE
    n_used = DEC_BATCH * n_pages
    n_pool = (5 * n_used + 3) // 4
    perm = jax.random.permutation(ks[4], n_pool)
    page_table = perm[:n_used].reshape(DEC_BATCH, n_pages).astype(jnp.int32)
    return {
        'x_prompt': nrm(0, (BATCH, SEQ, D_MODEL), 1.0),
        'x_sample': nrm(1, (DEC_BATCH, DEC_SEQ, D_MODEL), 1.0),
        'cache_k': nrm(2, (N_EVEN, n_pool, PAGE_SIZE, H_A, DH_A), 1.0),
        'cache_v': nrm(3, (N_EVEN, n_pool, PAGE_SIZE, H_A, DH_A), 1.0),
        'page_table': page_table,
        'state_wkv': nrm(5, (N_EVEN, DEC_BATCH, H_B, DH_B, DH_B), 0.3),
        'state_shift': nrm(6, (N_EVEN, DEC_BATCH, B_COLS), 1.0),
        'state_mlstm_c': nrm(7, (N_ODD, DEC_BATCH, H_C, DK_C, DV_C), 1.0),
        'state_mlstm_n': nrm(8, (N_ODD, DEC_BATCH, H_C, DK_C), 1.0),
        'state_mlstm_m': nrm(9, (N_ODD, DEC_BATCH, H_C), 1.0),
        'state_conv': nrm(10, (N_ODD, DEC_BATCH, CONV_W - 1, 2 * QK_C), 1.0),
        'meta_tokens': nrm(11, (N_META, D_MODEL), 1.0),
        'g_mix_pre': 1.0 + nrm(12, (DEPTH, D_MODEL), 0.02),
        'g_mix_post': 1.0 + nrm(13, (DEPTH, D_MODEL), 0.02),
        'g_ffn_pre': 1.0 + nrm(14, (DEPTH, D_MODEL), 0.02),
        'g_ffn_post': 1.0 + nrm(15, (DEPTH, D_MODEL), 0.02),
        'w_in_ab': nrm(16, (N_EVEN, D_MODEL, AB_COLS), D_MODEL ** -0.5),
        'w_out_ab': nrm(17, (N_EVEN, W_A + W_B, D_MODEL), (W_A + W_B) ** -0.5),
        'sb_bias': -jnp.linspace(SB_BIAS_LO, SB_BIAS_HI, H_A, dtype=f32)[None, :] + nrm(39, (N_EVEN, H_A), 0.1),
        'rwkv_mu': jax.random.uniform(ks[18], (N_EVEN, B_COLS), f32, 0.0, 1.0),
        'rwkv_w0': jax.random.uniform(ks[19], (N_EVEN, W_B), f32, -5.0, 1.0),
        'rwkv_w_decay_up': nrm(20, (N_EVEN, R_DECAY, W_B), R_DECAY ** -0.5),
        'rwkv_a0': nrm(21, (N_EVEN, W_B), 0.1),
        'rwkv_w_iclr_up': nrm(22, (N_EVEN, R_ICLR, W_B), R_ICLR ** -0.5),
        'rwkv_w_gate_up': nrm(23, (N_EVEN, R_GATE, W_B), R_GATE ** -0.5),
        'rwkv_k_k': 0.85 + nrm(24, (N_EVEN, W_B), 0.02),
        'rwkv_k_a': 1.0 + nrm(25, (N_EVEN, W_B), 0.02),
        'rwkv_r_k': nrm(26, (N_EVEN, W_B), 0.1),
        'rwkv_ln_w': 1.0 + nrm(27, (N_EVEN, W_B), 0.02),
        'rwkv_ln_b': nrm(28, (N_EVEN, W_B), 0.01),
        'w_in_c': nrm(29, (N_ODD, D_MODEL, C_COLS), D_MODEL ** -0.5),
        'conv_w': nrm(30, (N_ODD, CONV_W, 2 * QK_C), CONV_W ** -0.5),
        'conv_b': nrm(31, (N_ODD, 2 * QK_C), 0.01),
        'mlstm_b_i': nrm(32, (N_ODD, H_C), 0.1),
        'mlstm_b_f': jnp.linspace(3.0, 6.0, H_C, dtype=f32)[None, :] + nrm(33, (N_ODD, H_C), 0.1),
        'mlstm_norm_w': 1.0 + nrm(34, (N_ODD, V_C), 0.02),
        'w_out_c': nrm(35, (N_ODD, V_C, D_MODEL), V_C ** -0.5),
        'w_ffn_gate': nrm(36, (DEPTH, D_MODEL, D_FF), D_MODEL ** -0.5),
        'w_ffn_up': nrm(37, (DEPTH, D_MODEL, D_FF), D_MODEL ** -0.5),
        'w_ffn_down': nrm(38, (DEPTH, D_FF, D_MODEL), D_FF ** -0.5),
    }


def reference(x_prompt, x_sample, cache_k, cache_v, page_table, state_wkv, state_shift, state_mlstm_c,
              state_mlstm_n, state_mlstm_m, state_conv, meta_tokens, g_mix_pre, g_mix_post, g_ffn_pre,
              g_ffn_post, w_in_ab, w_out_ab, sb_bias, rwkv_mu, rwkv_w0, rwkv_w_decay_up, rwkv_a0, rwkv_w_iclr_up,
              rwkv_w_gate_up, rwkv_k_k, rwkv_k_a, rwkv_r_k, rwkv_ln_w, rwkv_ln_b, w_in_c, conv_w, conv_b,
              mlstm_b_i, mlstm_b_f, mlstm_norm_w, w_out_c, w_ffn_gate, w_ffn_up, w_ffn_down):
    P = {'g_mix_pre': g_mix_pre, 'g_mix_post': g_mix_post, 'g_ffn_pre': g_ffn_pre, 'g_ffn_post': g_ffn_post,
         'w_in_ab': w_in_ab, 'w_out_ab': w_out_ab, 'rwkv_mu': rwkv_mu, 'rwkv_w0': rwkv_w0,
         'rwkv_w_decay_up': rwkv_w_decay_up, 'rwkv_a0': rwkv_a0, 'rwkv_w_iclr_up': rwkv_w_iclr_up,
         'rwkv_w_gate_up': rwkv_w_gate_up, 'rwkv_k_k': rwkv_k_k, 'rwkv_k_a': rwkv_k_a, 'rwkv_r_k': rwkv_r_k,
         'rwkv_ln_w': rwkv_ln_w, 'rwkv_ln_b': rwkv_ln_b, 'w_in_c': w_in_c, 'conv_w': conv_w, 'conv_b': conv_b,
         'mlstm_b_i': mlstm_b_i, 'mlstm_b_f': mlstm_b_f, 'mlstm_norm_w': mlstm_norm_w, 'w_out_c': w_out_c,
         'w_ffn_gate': w_ffn_gate, 'w_ffn_up': w_ffn_up, 'w_ffn_down': w_ffn_down}
    f32 = jnp.float32
    b = x_prompt.shape[0]
    xp = jnp.concatenate([jnp.broadcast_to(meta_tokens.astype(x_prompt.dtype)[None], (b, N_META, D_MODEL)), x_prompt], axis=1)
    (yp, k_p, v_p, wkv_p, shift_p, c_p, n_p, m_p, conv_p) = run_trunk(
        xp, True,
        jnp.zeros((N_EVEN, b, H_B, DH_B, DH_B), f32),
        jnp.zeros((N_EVEN, b, B_COLS), x_prompt.dtype),
        jnp.zeros((N_ODD, b, H_C, DK_C, DV_C), f32),
        jnp.zeros((N_ODD, b, H_C, DK_C), f32),
        jnp.zeros((N_ODD, b, H_C), f32),
        jnp.zeros((N_ODD, b, CONV_W - 1, 2 * QK_C), x_prompt.dtype),
        lambda i, q, k, v: sb_prompt(q, k, v, sb_bias[i]), P)
    y_prompt = yp[:, N_META:]
    (y_sample, k_s, v_s, wkv_s, shift_s, c_s, n_s, m_s, conv_s) = run_trunk(
        x_sample, False, state_wkv, state_shift, state_mlstm_c, state_mlstm_n, state_mlstm_m, state_conv,
        lambda i, q, k, v: sb_sample(q, k, v, cache_k[i], cache_v[i], page_table, sb_bias[i]), P)
    return (y_prompt, y_sample, k_p, v_p, wkv_p, shift_p, c_p, n_p, m_p, conv_p,
            k_s, v_s, wkv_s, shift_s, c_s, n_s, m_s, conv_s)
```

```python
import functools

import jax
import jax.numpy as jnp
from jax import lax
from jax.experimental import pallas as pl
from jax.experimental.pallas import tpu as pltpu

F32 = jnp.float32
BF16 = jnp.bfloat16

D_MODEL = 1024
N_META = 16
H_A, DH_A = 8, 64
W_A = H_A * DH_A
H_B, DH_B = 8, 64
W_B = H_B * DH_B
R_DECAY, R_ICLR, R_GATE = 64, 64, 128
B_COLS = 3 * W_B + R_DECAY + R_ICLR + R_GATE
AB_COLS = 3 * W_A + B_COLS
H_C, DK_C, DV_C = 4, 128, 256
QK_C = H_C * DK_C
V_C = H_C * DV_C
CONV_W = 4
C_COLS_PAD = 2 * QK_C + 2 * V_C + 128
RMS_EPS = 1e-6
GN_EPS = 64e-5

LANE = 128
SUB = 8
CHUNK = 128
PAD_FRONT = CHUNK - N_META
VMEM_LIMIT = 56 * 1024 * 1024

NN = (((1,), (0,)), ((), ()))
NT = (((1,), (1,)), ((), ()))
TN = (((0,), (0,)), ((), ()))


def _cparams(sem):
    return pltpu.CompilerParams(dimension_semantics=sem, vmem_limit_bytes=VMEM_LIMIT)


def _dg(a, b, dims=NN):
    return lax.dot_general(a, b, dims, preferred_element_type=F32)


def _split(x):
    hi = x.astype(BF16)
    lo = (x - hi.astype(F32)).astype(BF16)
    return hi, lo


def _mm(a, b, dims=NN):
    return _dg(a.astype(BF16), b.astype(BF16), dims)


def _mm3(a, b, dims=NN):
    ah, al = _split(a)
    bh, bl = _split(b)
    return _dg(ah, bh, dims) + (_dg(ah, bl, dims) + _dg(al, bh, dims))


def _mm2l(a, b_bf16, dims=NN):
    ah, al = _split(a)
    return _dg(ah, b_bf16, dims) + _dg(al, b_bf16, dims)


def _rms(x, g):
    ms = jnp.mean(x * x, axis=-1, keepdims=True)
    return x * lax.rsqrt(ms + RMS_EPS) * g


def _softplus(z):
    return jnp.maximum(z, 0.0) + jnp.log1p(jnp.exp(-jnp.abs(z)))


def _pick_tm(n, cap=512):
    for tm in (512, 384, 256, 128, 64, 32, 16, 8):
        if tm <= cap and n % tm == 0:
            return tm
    raise ValueError(f"row count {n} not a multiple of 8")


def _pad_rows(x, rows):
    if x.shape[0] == rows:
        return x
    return jnp.concatenate([x, jnp.zeros((rows - x.shape[0],) + x.shape[1:], x.dtype)], axis=0)


def _norm_matmul_kernel(x_ref, g_ref, w_ref, o_ref):
    h = _rms(x_ref[...], g_ref[...]).astype(BF16)
    o_ref[...] = _dg(h, w_ref[...])


def norm_matmul(x, g, w):
    n, d = x.shape
    c = w.shape[1]
    tm = _pick_tm(n)
    return pl.pallas_call(
        _norm_matmul_kernel,
        grid=(n // tm,),
        in_specs=[pl.BlockSpec((tm, d), lambda i: (i, 0)),
                  pl.BlockSpec((1, d), lambda i: (0, 0)),
                  pl.BlockSpec((d, c), lambda i: (0, 0))],
        out_specs=pl.BlockSpec((tm, c), lambda i: (i, 0)),
        out_shape=jax.ShapeDtypeStruct((n, c), F32),
        compiler_params=_cparams(("parallel",)),
        name="norm_matmul",
    )(x, g, w)


def _proj_res_kernel(*refs, n_in):
    ins, ws = refs[:n_in], refs[n_in:2 * n_in]
    g_ref, x_ref, o_ref = refs[2 * n_in:]
    acc = _dg(ins[0][...].astype(BF16), ws[0][...])
    for a, w in zip(ins[1:], ws[1:]):
        acc = acc + _dg(a[...].astype(BF16), w[...])
    o_ref[...] = x_ref[...] + _rms(acc, g_ref[...])


def proj_res(ins, ws, g, x):
    n, d = x.shape
    tm = _pick_tm(n)
    n_in = len(ins)
    in_specs = ([pl.BlockSpec((tm, a.shape[1]), lambda i: (i, 0)) for a in ins]
                + [pl.BlockSpec(w.shape, lambda i: (0, 0)) for w in ws]
                + [pl.BlockSpec((1, d), lambda i: (0, 0)), pl.BlockSpec((tm, d), lambda i: (i, 0))])
    return pl.pallas_call(
        functools.partial(_proj_res_kernel, n_in=n_in),
        grid=(n // tm,),
        in_specs=in_specs,
        out_specs=pl.BlockSpec((tm, d), lambda i: (i, 0)),
        out_shape=jax.ShapeDtypeStruct((n, d), F32),
        compiler_params=_cparams(("parallel",)),
        name="proj_res",
    )(*ins, *ws, g, x)


def _ffn_kernel(x_ref, gpre_ref, wg_ref, wu_ref, wd_ref, gpost_ref, o_ref, *, n_split):
    x = x_ref[...]
    h = _rms(x, gpre_ref[...]).astype(BF16)
    d_ff = wg_ref.shape[1]
    step = d_ff // n_split
    ff = None
    for s in range(n_split):
        gate = _dg(h, wg_ref[:, s * step:(s + 1) * step])
        up = _dg(h, wu_ref[:, s * step:(s + 1) * step])
        a = (gate * jax.nn.sigmoid(gate) * up).astype(BF16)
        part = _dg(a, wd_ref[s * step:(s + 1) * step, :])
        ff = part if ff is None else ff + part
    o_ref[...] = x + _rms(ff, gpost_ref[...])


def ffn(x, gpre, wg, wu, wd, gpost):
    n, d = x.shape
    d_ff = wg.shape[1]
    tm = _pick_tm(n, 256)
    n_split = 2 if d_ff % (2 * LANE) == 0 else 1
    const = lambda i: (0, 0)
    return pl.pallas_call(
        functools.partial(_ffn_kernel, n_split=n_split),
        grid=(n // tm,),
        in_specs=[pl.BlockSpec((tm, d), lambda i: (i, 0)),
                  pl.BlockSpec((1, d), const),
                  pl.BlockSpec((d, d_ff), const),
                  pl.BlockSpec((d, d_ff), const),
                  pl.BlockSpec((d_ff, d), const),
                  pl.BlockSpec((1, d), const)],
        out_specs=pl.BlockSpec((tm, d), lambda i: (i, 0)),
        out_shape=jax.ShapeDtypeStruct((n, d), F32),
        compiler_params=_cparams(("parallel",)),
        name="ffn",
    )(x, gpre, wg, wu, wd, gpost)


def _later_matrix():
    j = lax.broadcasted_iota(jnp.int32, (CHUNK, 2 * CHUNK), 0)
    s = lax.broadcasted_iota(jnp.int32, (CHUNK, 2 * CHUNK), 1)
    return jnp.where((s >= CHUNK) | (j > s), 1.0, 0.0).astype(BF16)


def _sb_block(qh, bias, k, v, r_prev, later, mask, scale):
    z = _dg(qh, k, NT) * scale + bias
    sp = _softplus(z)
    lk = -sp
    lb = z - sp
    if mask is not None:
        lk = jnp.where(mask, lk, 0.0)
    hi, lo = _split(lk)
    cs = _dg(hi, later) + _dg(lo, later)
    att = jnp.exp(lb + cs[:, :CHUNK] + r_prev)
    if mask is not None:
        att = jnp.where(mask, att, 0.0)
    return _dg(att.astype(BF16), v), cs[:, CHUNK:]


def _sb_prompt_kernel(bias_ref, q_ref, k_ref, v_ref, o_ref, acc_ref, ra_ref, rb_ref, *, first_valid, scale):
    hp = pl.program_id(1)
    qi = pl.program_id(2)
    tq = q_ref.shape[1]
    q = q_ref[0]
    lane = lax.broadcasted_iota(jnp.int32, (1, LANE), 1)
    head_a = lane < DH_A
    qa = jnp.where(head_a, q, 0.0).astype(BF16)
    qb = jnp.where(head_a, 0.0, q).astype(BF16)
    bias_a = bias_ref[2 * hp]
    bias_b = bias_ref[2 * hp + 1]
    later = _later_matrix()
    acc_ref[...] = jnp.zeros_like(acc_ref)
    ra_ref[...] = jnp.zeros_like(ra_ref)
    rb_ref[...] = jnp.zeros_like(rb_ref)

    def process(kj, masked, zero_pad_v):
        start = pl.multiple_of(kj * CHUNK, CHUNK)
        k = k_ref[0, pl.ds(start, CHUNK), :].astype(BF16)
        v = v_ref[0, pl.ds(start, CHUNK), :]
        if zero_pad_v:
            krow = kj * CHUNK + lax.broadcasted_iota(jnp.int32, (CHUNK, 1), 0)
            v = jnp.where(krow >= first_valid, v, 0.0)
        v = v.astype(BF16)
        mask = None
        if masked:
            t_pos = qi * tq + lax.broadcasted_iota(jnp.int32, (tq, CHUNK), 0)
            s_pos = kj * CHUNK + lax.broadcasted_iota(jnp.int32, (tq, CHUNK), 1)
            mask = s_pos < t_pos
        pa, sa = _sb_block(qa, bias_a, k, v, ra_ref[...], later, mask, scale)
        pb, sb = _sb_block(qb, bias_b, k, v, rb_ref[...], later, mask, scale)
        acc_ref[...] += jnp.where(head_a, pa, pb)
        ra_ref[...] += sa
        rb_ref[...] += sb

    n_diag = tq // CHUNK
    for d in range(n_diag):
        process(qi * n_diag + (n_diag - 1 - d), True, True)

    def body(i, carry):
        process(qi * n_diag - 1 - i, False, False)
        return carry

    lax.fori_loop(0, qi * n_diag - 1, body, 0)

    @pl.when(qi > 0)
    def _():
        process(0, False, True)

    o_ref[0] = acc_ref[...]


def sb_prompt(proj, bias, first_valid):
    b, t, _ = proj.shape
    tq = CHUNK
    n_hp = H_A // 2
    kern = functools.partial(_sb_prompt_kernel, first_valid=first_valid, scale=DH_A ** -0.5)
    return pl.pallas_call(
        kern,
        grid=(b, n_hp, t // tq),
        in_specs=[pl.BlockSpec(memory_space=pltpu.SMEM),
                  pl.BlockSpec((1, tq, LANE), lambda bi, h, qi: (bi, qi, h)),
                  pl.BlockSpec((1, t, LANE), lambda bi, h, qi: (bi, 0, n_hp + h)),
                  pl.BlockSpec((1, t, LANE), lambda bi, h, qi: (bi, 0, 2 * n_hp + h))],
        out_specs=pl.BlockSpec((1, tq, LANE), lambda bi, h, qi: (bi, qi, h)),
        out_shape=jax.ShapeDtypeStruct((b, t, W_A), F32),
        scratch_shapes=[pltpu.VMEM((tq, LANE), F32)] * 3,
        compiler_params=_cparams(("parallel", "parallel", "arbitrary")),
        name="sb_prompt",
    )(bias, proj, proj, proj)


def _sb_sample_kernel(pt_ref, q_ref, kn_ref, vn_ref, bias_ref, kp_ref, vp_ref, o_ref,
                      qbd_ref, acc_ref, r_ref, *, scale):
    p = pl.program_id(1)
    n_pages = pl.num_programs(1)
    tq = q_ref.shape[1]
    rows = H_A * tq
    later = _later_matrix()
    col_head = lax.broadcasted_iota(jnp.int32, (1, W_A), 1) >> 6

    def block(k, v, mask):
        z = _dg(qbd_ref[...], k, NT) * scale + bias_ref[...]
        sp = _softplus(z)
        lk = -sp
        lb = z - sp
        if mask is not None:
            lk = jnp.where(mask, lk, 0.0)
        hi, lo = _split(lk)
        cs = _dg(hi, later) + _dg(lo, later)
        att = jnp.exp(lb + cs[:, :CHUNK] + r_ref[...])
        if mask is not None:
            att = jnp.where(mask, att, 0.0)
        acc_ref[...] += _dg(att.astype(BF16), v)
        r_ref[...] += cs[:, CHUNK:]

    @pl.when(p == 0)
    def _():
        q = q_ref[0]
        qbd_ref[...] = jnp.concatenate(
            [jnp.where(col_head == h, q, 0.0) for h in range(H_A)], axis=0).astype(BF16)
        acc_ref[...] = jnp.zeros_like(acc_ref)
        r_ref[...] = jnp.zeros_like(r_ref)
        kn = _pad_rows(kn_ref[0], CHUNK).astype(BF16)
        vn = _pad_rows(vn_ref[0], CHUNK).astype(BF16)
        t_pos = lax.broadcasted_iota(jnp.int32, (rows, CHUNK), 0) & (tq - 1)
        s_pos = lax.broadcasted_iota(jnp.int32, (rows, CHUNK), 1)
        block(kn, vn, s_pos < t_pos)

    block(kp_ref[0].astype(BF16), vp_ref[0].astype(BF16), None)

    @pl.when(p == n_pages - 1)
    def _():
        acc = acc_ref[...]
        out = jnp.where(col_head == 0, acc[0:tq], 0.0)
        for h in range(1, H_A):
            out = out + jnp.where(col_head == h, acc[h * tq:(h + 1) * tq], 0.0)
        o_ref[0] = out


def sb_sample(proj, pool_k, pool_v, page_table, bias):
    db, tq, _ = proj.shape
    n_pages = page_table.shape[1]
    page = pool_k.shape[1]
    assert page == CHUNK and tq % SUB == 0 and tq <= CHUNK and tq & (tq - 1) == 0 and DH_A == 64 and DH_B == 64
    rows = H_A * tq
    bias_rows = jnp.broadcast_to(jnp.repeat(bias, tq)[:, None], (rows, CHUNK)).astype(F32)
    pt = page_table.reshape(-1).astype(jnp.int32)

    def page_map(bi, p, pt_ref):
        return (pt_ref[bi * n_pages + (n_pages - 1 - p)], 0, 0)

    grid_spec = pltpu.PrefetchScalarGridSpec(
        num_scalar_prefetch=1,
        grid=(db, n_pages),
        in_specs=[pl.BlockSpec((1, tq, W_A), lambda bi, p, pt_ref: (bi, 0, 0)),
                  pl.BlockSpec((1, tq, W_A), lambda bi, p, pt_ref: (bi, 0, 1)),
                  pl.BlockSpec((1, tq, W_A), lambda bi, p, pt_ref: (bi, 0, 2)),
                  pl.BlockSpec((rows, CHUNK), lambda bi, p, pt_ref: (0, 0)),
                  pl.BlockSpec((1, page, W_A), page_map),
                  pl.BlockSpec((1, page, W_A), page_map)],
        out_specs=pl.BlockSpec((1, tq, W_A), lambda bi, p, pt_ref: (bi, 0, 0)),
        scratch_shapes=[pltpu.VMEM((rows, W_A), BF16),
                        pltpu.VMEM((rows, W_A), F32),
                        pltpu.VMEM((rows, CHUNK), F32)],
    )
    return pl.pallas_call(
        functools.partial(_sb_sample_kernel, scale=DH_A ** -0.5),
        grid_spec=grid_spec,
        out_shape=jax.ShapeDtypeStruct((db, tq, W_A), F32),
        compiler_params=_cparams(("parallel", "arbitrary")),
        name="sb_sample",
    )(pt, proj, proj, proj, bias_rows, pool_k, pool_v)


def _unit_lower_inverse(n_mat, row, col):
    eye = jnp.where(row == col, 1.0, 0.0)
    log_base = 4
    d = jnp.where((row >> log_base) == (col >> log_base), n_mat, 0.0)
    x = eye - d
    dp = d
    for _ in range(log_base - 1):
        dp = _mm(dp, dp)
        x = x + _mm(x, dp)
    lg = log_base
    while (1 << lg) < CHUNK:
        off = (row >> (lg + 1)) == (col >> (lg + 1))
        off = jnp.where(off, (row >> lg) & 1, 0) > ((col >> lg) & 1)
        bmat = jnp.where(off, n_mat, 0.0)
        x = x - _mm(x, _mm(bmat, x))
        lg += 1
    resid = (eye - x) - _mm3(n_mat, x)
    return x + _mm(x, resid)


def _rwkv_kernel(r_ref, k_ref, v_ref, da_ref, g_ref, s0_ref, hb0_ref, p_ref, mu2_ref,
                 wd_ref, wa_ref, wg_ref, o_ref, hb_out_ref, hb_ref, carry_ref,
                 *, first_valid, t_valid):
    c = pl.program_id(2)
    n_c = pl.num_programs(2)
    c_in = r_ref.shape[1]
    row = lax.broadcasted_iota(jnp.int32, (CHUNK, CHUNK), 0)
    col = lax.broadcasted_iota(jnp.int32, (CHUNK, CHUNK), 1)
    rcol = lax.broadcasted_iota(jnp.int32, (CHUNK, 1), 0)
    rglob = c * CHUNK + rcol
    valid = (rglob >= first_valid) & (rglob < t_valid)
    head_a = lax.broadcasted_iota(jnp.int32, (1, LANE), 1) < DH_B
    same_head = (row >> 6) == (col >> 6)
    seg_sum = jnp.where(same_head, 1.0, 0.0).astype(BF16)
    seg_avg = jnp.where(same_head, 1.0 / DH_B, 0.0).astype(BF16)

    @pl.when(c == 0)
    def _():
        hb_ref[...] = hb0_ref[0, 0]
        for j in range(5):
            carry_ref[j] = jnp.broadcast_to(s0_ref[0, 0, 0:1, j * LANE:(j + 1) * LANE], (SUB, LANE))

    P = p_ref[0]
    mu2 = mu2_ref[...]
    mus = (P[0:1], P[1:2], P[2:3], mu2[0:1], mu2[1:2])
    us = []
    for j, (ref, mu) in enumerate(zip((r_ref, k_ref, v_ref, da_ref, g_ref), mus)):
        x = _pad_rows(ref[0], CHUNK)
        s0_row = s0_ref[0, 0, 0:1, j * LANE:(j + 1) * LANE]
        prev = jnp.where(rcol == 0, carry_ref[j][SUB - 1:SUB, :], pltpu.roll(x, 1, axis=0))
        prev = jnp.where(rglob == first_valid, s0_row, prev)
        us.append(x + (prev - x) * mu)
        carry_ref[j] = x[CHUNK - SUB:CHUNK, :]
    r, k, v, u_da, u_g = us
    w0, a0, k_k, k_a, r_k, ln_w, ln_b = (P[i:i + 1] for i in range(3, 10))

    xw = w0 + _mm(jnp.tanh(u_da), wd_ref[...])
    w_log = -_softplus(-xw) - 0.5
    lw = jnp.where(valid, -jnp.exp(w_log), 0.0)
    a = jax.nn.sigmoid(a0 + _mm(u_da, wa_ref[...]))
    g = _mm(jax.nn.sigmoid(u_g), wg_ref[...])
    kk = k * k_k
    kk = kk / jnp.maximum(jnp.sqrt(_mm2l(kk * kk, seg_sum)), 1e-12)
    k_mod = k * (1.0 + (a - 1.0) * k_a)
    coef = _mm2l(r * k_mod * r_k, seg_sum)
    k_mod = jnp.where(valid, k_mod, 0.0)
    b = jnp.where(valid, kk * a, 0.0)

    incl = jnp.where(col <= row, 1.0, 0.0).astype(BF16)
    lw_hi = lw.astype(BF16)
    lw_mid = (lw - lw_hi.astype(F32))
    lw_mid_b = lw_mid.astype(BF16)
    lw_lo = (lw_mid - lw_mid_b.astype(F32)).astype(BF16)
    cum = _dg(incl, lw_hi) + (_dg(incl, lw_mid_b) + _dg(incl, lw_lo))
    cum_end = cum[CHUNK - 1:CHUNK, :]
    a_t = kk * jnp.exp(cum - lw)
    r_t = r * jnp.exp(cum)
    inv_p = jnp.exp(-cum)
    b_t = b * inv_p
    k_t = k_mod * inv_p
    to_end = jnp.exp(cum_end - cum)
    k_end = k_mod * to_end
    b_end = b * to_end
    p_end_col = jnp.exp(jnp.sum(lw.T, axis=1, keepdims=True))

    hb = hb_ref[...]
    strict = col < row
    incl_m = col <= row
    lhs2 = jnp.concatenate([a_t, r_t], axis=0)
    parts = []
    for head_mask in (head_a, jnp.logical_not(head_a)):
        l2 = jnp.where(head_mask, lhs2, 0.0)
        m_b = _mm3(l2, b_t, NT)
        m_k = _mm3(l2, k_t, NT)
        n_mat = jnp.where(strict, m_b[:CHUNK], 0.0)
        a_ak = jnp.where(strict, m_k[:CHUNK], 0.0)
        a_rb = jnp.where(incl_m, m_b[CHUNK:], 0.0)
        a_rk = jnp.where(incl_m, m_k[CHUNK:], 0.0)
        parts.append((_unit_lower_inverse(n_mat, row, col), a_ak, a_rb, a_rk))

    def sel(xa, xb):
        return jnp.where(head_a, xa, xb)

    rhs = _mm3(a_t, hb) + sel(_mm3(parts[0][1], v), _mm3(parts[1][1], v))
    u = sel(_mm3(parts[0][0], rhs), _mm3(parts[1][0], rhs))
    vu = jnp.concatenate([v, u], axis=0)
    y = _mm3(r_t, hb) + sel(
        _mm3(jnp.concatenate([parts[0][3], -parts[0][2]], axis=1), vu),
        _mm3(jnp.concatenate([parts[1][3], -parts[1][2]], axis=1), vu))
    upd = _mm3(jnp.concatenate([k_end, -b_end], axis=0), vu, TN)
    hb_ref[...] = hb * p_end_col + jnp.where(same_head, upd, 0.0)

    mean = _mm2l(y, seg_avg)
    yc = y - mean
    var = _mm2l(yc * yc, seg_avg)
    yn = yc * lax.rsqrt(var + GN_EPS) * ln_w + ln_b
    out = (yn + coef * v) * g
    o_ref[0] = out[:c_in]

    @pl.when(c == n_c - 1)
    def _():
        hb_out_ref[0, 0] = hb_ref[...]


def rwkv(proj, shift0, hb0, pair_params, mu2, wd, wa, wg, first_valid, t_valid):
    bsz, t, _ = proj.shape
    n_pr = H_B // 2
    c_in = min(t, CHUNK)
    n_c = pl.cdiv(t, CHUNK)
    assert t % c_in == 0
    q0 = 3 * W_A // LANE
    w_blk = W_B // LANE

    def proj_spec(off, per_pair):
        if per_pair:
            return pl.BlockSpec((1, c_in, LANE), lambda bi, pr, c: (bi, c, q0 + off + pr))
        return pl.BlockSpec((1, c_in, LANE), lambda bi, pr, c: (bi, c, q0 + off))

    kern = functools.partial(_rwkv_kernel, first_valid=first_valid, t_valid=t_valid)
    return pl.pallas_call(
        kern,
        grid=(bsz, n_pr, n_c),
        in_specs=[proj_spec(0, True), proj_spec(w_blk, True), proj_spec(2 * w_blk, True),
                  proj_spec(3 * w_blk, False), proj_spec(3 * w_blk + 1, False),
                  pl.BlockSpec((1, 1, SUB, 5 * LANE), lambda bi, pr, c: (bi, pr, 0, 0)),
                  pl.BlockSpec((1, 1, CHUNK, CHUNK), lambda bi, pr, c: (bi, pr, 0, 0)),
                  pl.BlockSpec((1, 16, LANE), lambda bi, pr, c: (pr, 0, 0)),
                  pl.BlockSpec((SUB, LANE), lambda bi, pr, c: (0, 0)),
                  pl.BlockSpec((LANE, LANE), lambda bi, pr, c: (0, pr)),
                  pl.BlockSpec((LANE, LANE), lambda bi, pr, c: (0, pr)),
                  pl.BlockSpec((LANE, LANE), lambda bi, pr, c: (0, pr))],
        out_specs=[pl.BlockSpec((1, c_in, LANE), lambda bi, pr, c: (bi, c, pr)),
                   pl.BlockSpec((1, 1, CHUNK, CHUNK), lambda bi, pr, c: (bi, pr, 0, 0))],
        out_shape=[jax.ShapeDtypeStruct((bsz, t, W_B), F32),
                   jax.ShapeDtypeStruct((bsz, n_pr, CHUNK, CHUNK), F32)],
        scratch_shapes=[pltpu.VMEM((CHUNK, CHUNK), F32), pltpu.VMEM((5, SUB, LANE), F32)],
        compiler_params=_cparams(("parallel", "parallel", "arbitrary")),
        name="rwkv",
    )(proj, proj, proj, proj, proj, shift0, hb0, pair_params, mu2, wd, wa, wg)


def _scan_rows(x, n):
    ridx = lax.broadcasted_iota(jnp.int32, (n, 1), 0)
    s = 1
    while s < n:
        x = x + jnp.where(ridx >= s, pltpu.roll(x, s, axis=0), 0.0)
        s *= 2
    return x


def _scan_lanes(x):
    lidx = lax.broadcasted_iota(jnp.int32, (1, LANE), 1)
    s = 1
    while s < LANE:
        x = x + jnp.where(lidx >= s, pltpu.roll(x, s, axis=1), 0.0)
        s *= 2
    return x


def _mlstm_kernel(q_ref, k_ref, v_ref, og_ref, gc_ref, gt_ref, conv0_ref, c0_ref, n0_ref, m0_ref,
                  cw_ref, cb_ref, gb_row_ref, gb_col_ref, nw_ref,
                  o_ref, c_out_ref, n_out_ref, m_out_ref,
                  cext_ref, m_ref, halo_ref, *, first_valid):
    c = pl.program_id(1)
    n_c = pl.num_programs(1)
    lq = q_ref.shape[1]
    rcol = lax.broadcasted_iota(jnp.int32, (lq, 1), 0)
    rglob = c * lq + rcol
    valid_c = rglob >= first_valid
    lrow = lax.broadcasted_iota(jnp.int32, (1, LANE), 1)
    valid_r = ((c * lq + lrow) >= first_valid) & (lrow < lq)
    causal = lax.broadcasted_iota(jnp.int32, (lq, LANE), 1) <= lax.broadcasted_iota(jnp.int32, (lq, LANE), 0)

    @pl.when(c == 0)
    def _():
        for h in range(H_C):
            cext_ref[h, :, :DV_C] = c0_ref[0, h]
            cext_ref[h, :, DV_C:] = n0_ref[0, h]
        m_ref[...] = m0_ref[0]
        halo_ref[...] = conv0_ref[0]

    x = jnp.where(valid_c, jnp.concatenate([q_ref[0], k_ref[0]], axis=1), 0.0)
    xcat = jnp.concatenate([halo_ref[...], x], axis=0)
    cw = cw_ref[...]
    conv = cb_ref[...] + x * cw[CONV_W - 1:CONV_W]
    for dlt in range(1, CONV_W):
        conv = conv + pltpu.roll(xcat, dlt, axis=0)[SUB:SUB + lq] * cw[CONV_W - 1 - dlt:CONV_W - dlt]
    halo_ref[...] = xcat[lq:lq + SUB]
    qk = conv * jax.nn.sigmoid(conv)

    gcol = gc_ref[0] + gb_row_ref[...]
    lane_is_f = (lrow >= H_C) & (lrow < 2 * H_C)
    gcol = jnp.where(lane_is_f, -_softplus(-gcol), gcol)
    gcol = jnp.where(valid_c, gcol, jnp.where(lane_is_f, 0.0, -jnp.inf))
    bcum_cols = _scan_rows(jnp.where(lane_is_f, gcol, 0.0), lq)
    grow = gt_ref[0] + gb_col_ref[...]
    srow = lax.broadcasted_iota(jnp.int32, (SUB, 1), 0)
    row_is_f = srow >= H_C
    grow = jnp.where(row_is_f, -_softplus(-grow), grow)
    grow = jnp.where(valid_r, grow, jnp.where(row_is_f, 0.0, -jnp.inf))
    bcum_rows = _scan_lanes(jnp.where(row_is_f, grow, 0.0))

    ones_ext = jnp.ones((lq, LANE), F32)
    for h in range(H_C):
        q = qk[:, h * DK_C:(h + 1) * DK_C]
        k = qk[:, QK_C + h * DK_C:QK_C + (h + 1) * DK_C] * (DK_C ** -0.5)
        v_ext = jnp.concatenate([v_ref[0, :, h * DV_C:(h + 1) * DV_C], ones_ext], axis=1)
        li_c = gcol[:, h:h + 1]
        bc_c = bcum_cols[:, H_C + h:H_C + h + 1]
        li_r = grow[h:h + 1, :]
        bc_r = bcum_rows[H_C + h:H_C + h + 1, :]
        m_prev = m_ref[h, 0:1, 0:1]
        cext = cext_ref[h]

        dmat = jnp.where(causal, bc_c - bc_r + li_r, -jnp.inf)
        inter = bc_c + m_prev
        m_row = jnp.maximum(inter, jnp.max(dmat, axis=1, keepdims=True))
        w_intra = jnp.exp(dmat - m_row)
        w_inter = jnp.exp(inter - m_row)
        k_pad = _pad_rows(k, CHUNK)
        s = _mm(q, k_pad, NT) * w_intra
        num_ext = _mm(s, _pad_rows(v_ext, CHUNK)) + w_inter * _mm(q, cext)
        den = num_ext[:, DV_C:DV_C + 1]
        hh = num_ext[:, :DV_C] / jnp.maximum(jnp.abs(den), jnp.exp(-m_row))

        m_new = m_row[lq - 1:lq, :]
        bc_last = bc_c[lq - 1:lq, :]
        w_end = jnp.exp(bc_last - bc_c + li_c - m_new)
        dec = jnp.exp(bc_last + m_prev - m_new)
        cext_ref[h] = dec * cext + _mm(k_pad, _pad_rows(w_end * v_ext, CHUNK), TN)
        m_ref[h] = jnp.broadcast_to(m_new, (SUB, LANE))

        mu = jnp.mean(hh, axis=1, keepdims=True)
        hc = hh - mu
        var = jnp.mean(hc * hc, axis=1, keepdims=True)
        hn = hc * lax.rsqrt(var + RMS_EPS) * nw_ref[:, h * DV_C:(h + 1) * DV_C]
        o_ref[0, :, h * DV_C:(h + 1) * DV_C] = hn * jax.nn.sigmoid(og_ref[0, :, h * DV_C:(h + 1) * DV_C])

    @pl.when(c == n_c - 1)
    def _():
        for h in range(H_C):
            c_out_ref[0, h] = cext_ref[h, :, :DV_C]
            n_out_ref[0, h] = cext_ref[h, :, DV_C:]
        m_out_ref[0] = m_ref[...]


def mlstm(proj, gates_t, conv0, c0, n0b, m0b, conv_w, conv_b, gb_row, gb_col, norm_w, first_valid):
    bsz, t, _ = proj.shape
    lq = min(t, CHUNK)
    n_c = t // lq
    assert t % lq == 0 and gates_t.shape[2] == n_c * CHUNK
    qk2 = 2 * QK_C
    kern = functools.partial(_mlstm_kernel, first_valid=first_valid)
    const2 = lambda bi, c: (0, 0)
    return pl.pallas_call(
        kern,
        grid=(bsz, n_c),
        in_specs=[pl.BlockSpec((1, lq, QK_C), lambda bi, c: (bi, c, 0)),
                  pl.BlockSpec((1, lq, QK_C), lambda bi, c: (bi, c, 1)),
                  pl.BlockSpec((1, lq, V_C), lambda bi, c: (bi, c, qk2 // V_C)),
                  pl.BlockSpec((1, lq, V_C), lambda bi, c: (bi, c, qk2 // V_C + 1)),
                  pl.BlockSpec((1, lq, LANE), lambda bi, c: (bi, c, (qk2 + 2 * V_C) // LANE)),
                  pl.BlockSpec((1, SUB, LANE), lambda bi, c: (bi, 0, c)),
                  pl.BlockSpec((1, SUB, qk2), lambda bi, c: (bi, 0, 0)),
                  pl.BlockSpec((1, H_C, DK_C, DV_C), lambda bi, c: (bi, 0, 0, 0)),
                  pl.BlockSpec((1, H_C, DK_C, LANE), lambda bi, c: (bi, 0, 0, 0)),
                  pl.BlockSpec((1, H_C, SUB, LANE), lambda bi, c: (bi, 0, 0, 0)),
                  pl.BlockSpec((SUB, qk2), const2),
                  pl.BlockSpec((1, qk2), const2),
                  pl.BlockSpec((1, LANE), const2),
                  pl.BlockSpec((SUB, LANE), const2),
                  pl.BlockSpec((1, V_C), const2)],
        out_specs=[pl.BlockSpec((1, lq, V_C), lambda bi, c: (bi, c, 0)),
                   pl.BlockSpec((1, H_C, DK_C, DV_C), lambda bi, c: (bi, 0, 0, 0)),
                   pl.BlockSpec((1, H_C, DK_C, LANE), lambda bi, c: (bi, 0, 0, 0)),
                   pl.BlockSpec((1, H_C, SUB, LANE), lambda bi, c: (bi, 0, 0, 0))],
        out_shape=[jax.ShapeDtypeStruct((bsz, t, V_C), F32),
                   jax.ShapeDtypeStruct((bsz, H_C, DK_C, DV_C), F32),
                   jax.ShapeDtypeStruct((bsz, H_C, DK_C, LANE), F32),
                   jax.ShapeDtypeStruct((bsz, H_C, SUB, LANE), F32)],
        scratch_shapes=[pltpu.VMEM((H_C, DK_C, DV_C + LANE), F32),
                        pltpu.VMEM((H_C, SUB, LANE), F32),
                        pltpu.VMEM((SUB, qk2), F32)],
        compiler_params=_cparams(("parallel", "arbitrary")),
        name="mlstm",
    )(proj, proj, proj, proj, proj, gates_t, conv0, c0, n0b, m0b, conv_w, conv_b, gb_row, gb_col, norm_w)


def _pad_to(x, axis, size):
    pad = [(0, 0)] * x.ndim
    pad[axis] = (0, size - x.shape[axis])
    return jnp.pad(x, pad)


def _rwkv_params(i, rwkv_mu, rwkv_w0, rwkv_a0, rwkv_k_k, rwkv_k_a, rwkv_r_k, rwkv_ln_w, rwkv_ln_b,
                 w_decay_up, w_iclr_up, w_gate_up):
    n_pr = H_B // 2
    mu = rwkv_mu[i]
    rows = [mu[0:W_B], mu[W_B:2 * W_B], mu[2 * W_B:3 * W_B], rwkv_w0[i], rwkv_a0[i], rwkv_k_k[i],
            rwkv_k_a[i], rwkv_r_k[i], rwkv_ln_w[i], rwkv_ln_b[i]]
    pp = jnp.stack([r.reshape(n_pr, LANE) for r in rows], axis=1)
    pp = _pad_to(pp, 1, 16)
    mu2 = _pad_to(mu[3 * W_B:].reshape(2, LANE), 0, SUB)
    wd = jnp.concatenate([w_decay_up[i], jnp.zeros((R_ICLR, W_B), F32)], axis=0).astype(BF16)
    wa = jnp.concatenate([jnp.zeros((R_DECAY, W_B), F32), w_iclr_up[i]], axis=0).astype(BF16)
    wg = w_gate_up[i].astype(BF16)
    return pp, mu2, wd, wa, wg


def _pair_shift(shift):
    n_pr = H_B // 2
    bsz = shift.shape[0]
    tiles = shift.reshape(bsz, B_COLS // LANE, LANE)
    w_blk = W_B // LANE
    per_pair = [jnp.stack([tiles[:, pr], tiles[:, w_blk + pr], tiles[:, 2 * w_blk + pr],
                           tiles[:, 3 * w_blk], tiles[:, 3 * w_blk + 1]], axis=1).reshape(bsz, 5 * LANE)
                for pr in range(n_pr)]
    s = jnp.stack(per_pair, axis=1)[:, :, None, :]
    return _pad_to(s, 2, SUB)


def _pair_states(wkv):
    bsz = wkv.shape[0]
    st = jnp.swapaxes(wkv, -1, -2).reshape(bsz, H_B // 2, 2, DH_B, DH_B)
    z = jnp.zeros_like(st[:, :, 0])
    top = jnp.concatenate([st[:, :, 0], z], axis=-1)
    bot = jnp.concatenate([z, st[:, :, 1]], axis=-1)
    return jnp.concatenate([top, bot], axis=-2)


def _unpair_states(hb):
    bsz = hb.shape[0]
    a = hb[:, :, :DH_B, :DH_B]
    b = hb[:, :, DH_B:, DH_B:]
    st = jnp.stack([a, b], axis=2).reshape(bsz, H_B, DH_B, DH_B)
    return jnp.swapaxes(st, -1, -2)


def kernel(x_prompt, x_sample, cache_k, cache_v, page_table, state_wkv, state_shift, state_mlstm_c, state_mlstm_n, state_mlstm_m, state_conv, meta_tokens, g_mix_pre, g_mix_post, g_ffn_pre, g_ffn_post, w_in_ab, w_out_ab, sb_bias, rwkv_mu, rwkv_w0, rwkv_w_decay_up, rwkv_a0, rwkv_w_iclr_up, rwkv_w_gate_up, rwkv_k_k, rwkv_k_a, rwkv_r_k, rwkv_ln_w, rwkv_ln_b, w_in_c, conv_w, conv_b, mlstm_b_i, mlstm_b_f, mlstm_norm_w, w_out_c, w_ffn_gate, w_ffn_up, w_ffn_down):
    bsz, seq, d = x_prompt.shape
    db, ds, _ = x_sample.shape
    depth = g_mix_pre.shape[0]
    tp = PAD_FRONT + N_META + seq
    assert seq % CHUNK == 0 and ds % SUB == 0 and ds <= CHUNK and seq >= CONV_W and ds >= CONV_W

    meta = jnp.broadcast_to(meta_tokens.astype(F32)[None], (bsz, N_META, d))
    xp = jnp.concatenate([jnp.zeros((bsz, PAD_FRONT, d), F32), meta, x_prompt], axis=1).reshape(bsz * tp, d)
    xs = x_sample.reshape(db * ds, d)

    n_pool = cache_k.shape[1]
    outs = {name: [] for name in ("k_p", "v_p", "wkv_p", "sh_p", "c_p", "n_p", "m_p", "cv_p",
                                  "k_s", "v_s", "wkv_s", "sh_s", "c_s", "n_s", "m_s", "cv_s")}
    row = lambda v: v.reshape(1, -1)

    for layer in range(depth):
        i = layer // 2
        if layer % 2 == 0:
            w_in = w_in_ab[i].astype(BF16)
            w_out = w_out_ab[i].astype(BF16)
            pp, mu2, wd, wa, wg = _rwkv_params(i, rwkv_mu, rwkv_w0, rwkv_a0, rwkv_k_k, rwkv_k_a, rwkv_r_k,
                                               rwkv_ln_w, rwkv_ln_b, rwkv_w_decay_up, rwkv_w_iclr_up, rwkv_w_gate_up)
            new_x = []
            for grp, x in (("p", xp), ("s", xs)):
                nb, t = (bsz, tp) if grp == "p" else (db, ds)
                proj = norm_matmul(x, row(g_mix_pre[layer]), w_in).reshape(nb, t, AB_COLS)
                if grp == "p":
                    o_a = sb_prompt(proj, sb_bias[i], PAD_FRONT)
                    shift0 = jnp.zeros((nb, H_B // 2, SUB, 5 * LANE), F32)
                    hb0 = jnp.zeros((nb, H_B // 2, CHUNK, CHUNK), F32)
                    first_valid = PAD_FRONT
                else:
                    o_a = sb_sample(proj, cache_k[i].reshape(n_pool, CHUNK, W_A),
                                    cache_v[i].reshape(n_pool, CHUNK, W_A), page_table, sb_bias[i])
                    shift0 = _pair_shift(state_shift[i])
                    hb0 = _pair_states(state_wkv[i])
                    first_valid = 0
                o_b, hb = rwkv(proj, shift0, hb0, pp, mu2, wd, wa, wg, first_valid, t)
                new_x.append(proj_res([o_a.reshape(nb * t, W_A), o_b.reshape(nb * t, W_B)],
                                      [w_out[:W_A], w_out[W_A:]], row(g_mix_post[layer]), x))
                outs["k_" + grp].append(proj[:, first_valid:, W_A:2 * W_A].reshape(nb, t - first_valid, H_A, DH_A))
                outs["v_" + grp].append(proj[:, first_valid:, 2 * W_A:3 * W_A].reshape(nb, t - first_valid, H_A, DH_A))
                outs["wkv_" + grp].append(_unpair_states(hb))
                outs["sh_" + grp].append(proj[:, t - 1, 3 * W_A:])
            xp, xs = new_x
        else:
            wc = w_in_c[i]
            ncol = wc.shape[1]
            wperm = jnp.concatenate([wc[:, :2 * QK_C + V_C], wc[:, ncol - V_C:],
                                     wc[:, 2 * QK_C + V_C:ncol - V_C]], axis=1)
            wperm = _pad_to(wperm, 1, C_COLS_PAD).astype(BF16)
            w_out = w_out_c[i].astype(BF16)
            cw = _pad_to(conv_w[i], 0, SUB)
            gb = jnp.concatenate([mlstm_b_i[i], mlstm_b_f[i]])
            gb_row = _pad_to(gb, 0, LANE).reshape(1, LANE)
            gb_col = jnp.broadcast_to(gb[:, None], (SUB, LANE))
            new_x = []
            for grp, x in (("p", xp), ("s", xs)):
                nb, t = (bsz, tp) if grp == "p" else (db, ds)
                proj = norm_matmul(x, row(g_mix_pre[layer]), wperm).reshape(nb, t, C_COLS_PAD)
                g0 = 2 * QK_C + 2 * V_C
                gates_t = jnp.swapaxes(proj[:, :, g0:g0 + SUB], 1, 2)
                if grp == "p":
                    first_valid = PAD_FRONT
                    conv0 = jnp.zeros((nb, SUB, 2 * QK_C), F32)
                    c0 = jnp.zeros((nb, H_C, DK_C, DV_C), F32)
                    n0 = jnp.zeros((nb, H_C, DK_C), F32)
                    m0 = jnp.zeros((nb, H_C), F32)
                else:
                    first_valid = 0
                    gates_t = _pad_to(gates_t, 2, CHUNK)
                    conv0 = jnp.concatenate([jnp.zeros((nb, SUB - (CONV_W - 1), 2 * QK_C), F32), state_conv[i]], axis=1)
                    c0, n0, m0 = state_mlstm_c[i], state_mlstm_n[i], state_mlstm_m[i]
                n0b = jnp.broadcast_to(n0[..., None], (nb, H_C, DK_C, LANE))
                m0b = jnp.broadcast_to(m0[..., None, None], (nb, H_C, SUB, LANE))
                hmix, c_new, n_new, m_new = mlstm(proj, gates_t, conv0, c0, n0b, m0b, cw, row(conv_b[i]),
                                                  gb_row, gb_col, row(mlstm_norm_w[i]), first_valid)
                new_x.append(proj_res([hmix.reshape(nb * t, V_C)], [w_out], row(g_mix_post[layer]), x))
                outs["c_" + grp].append(c_new)
                outs["n_" + grp].append(n_new[..., 0])
                outs["m_" + grp].append(m_new[..., 0, 0])
                outs["cv_" + grp].append(proj[:, t - (CONV_W - 1):, :2 * QK_C])
            xp, xs = new_x
        wgt, wup, wdn = (w_ffn_gate[layer].astype(BF16), w_ffn_up[layer].astype(BF16),
                         w_ffn_down[layer].astype(BF16))
        xp = ffn(xp, row(g_ffn_pre[layer]), wgt, wup, wdn, row(g_ffn_post[layer]))
        xs = ffn(xs, row(g_ffn_pre[layer]), wgt, wup, wdn, row(g_ffn_post[layer]))

    y_prompt = xp.reshape(bsz, tp, d)[:, PAD_FRONT + N_META:]
    y_sample = xs.reshape(db, ds, d)
    st = lambda name: jnp.stack(outs[name])
    return (y_prompt, y_sample, st("k_p"), st("v_p"), st("wkv_p"), st("sh_p"), st("c_p"), st("n_p"),
            st("m_p"), st("cv_p"), st("k_s"), st("v_s"), st("wkv_s"), st("sh_s"), st("c_s"), st("n_s"),
            st("m_s"), st("cv_s"))
```

```python
import functools

import jax
import jax.numpy as jnp
from jax import lax
from jax.experimental import pallas as pl
from jax.experimental.pallas import tpu as pltpu

F32 = jnp.float32
BF16 = jnp.bfloat16

D_MODEL = 1024
N_META = 16
H_A, DH_A = 8, 64
W_A = H_A * DH_A
H_B, DH_B = 8, 64
W_B = H_B * DH_B
R_DECAY, R_ICLR, R_GATE = 64, 64, 128
B_COLS = 3 * W_B + R_DECAY + R_ICLR + R_GATE
AB_COLS = 3 * W_A + B_COLS
H_C, DK_C, DV_C = 4, 128, 256
QK_C = H_C * DK_C
V_C = H_C * DV_C
CONV_W = 4
C_COLS_PAD = 2 * QK_C + 2 * V_C + 128
RMS_EPS = 1e-6
GN_EPS = 64e-5

LANE = 128
SUB = 8
CHUNK = 128
PAD_FRONT = CHUNK - N_META
VMEM_LIMIT = 56 * 1024 * 1024

NN = (((1,), (0,)), ((), ()))
NT = (((1,), (1,)), ((), ()))
TN = (((0,), (0,)), ((), ()))


def _cparams(sem):
    return pltpu.CompilerParams(dimension_semantics=sem, vmem_limit_bytes=VMEM_LIMIT)


def _dg(a, b, dims=NN):
    return lax.dot_general(a, b, dims, preferred_element_type=F32)


def _split(x):
    hi = x.astype(BF16)
    lo = (x - hi.astype(F32)).astype(BF16)
    return hi, lo


def _mm(a, b, dims=NN):
    return _dg(a.astype(BF16), b.astype(BF16), dims)


def _mm3(a, b, dims=NN):
    ah, al = _split(a)
    bh, bl = _split(b)
    return _dg(ah, bh, dims) + (_dg(ah, bl, dims) + _dg(al, bh, dims))


def _mm2l(a, b_bf16, dims=NN):
    ah, al = _split(a)
    return _dg(ah, b_bf16, dims) + _dg(al, b_bf16, dims)


_mm_tok = _mm
_mm_state = _mm


def _rms(x, g):
    ms = jnp.mean(x * x, axis=-1, keepdims=True)
    return x * lax.rsqrt(ms + RMS_EPS) * g


def _softplus(z):
    return jnp.maximum(z, 0.0) + jnp.log1p(jnp.exp(-jnp.abs(z)))


def _pick_tm(n, cap=512):
    for tm in (512, 384, 256, 128, 64, 32, 16, 8):
        if tm <= cap and n % tm == 0:
            return tm
    raise ValueError(f"row count {n} not a multiple of 8")


def _pad_rows(x, rows):
    if x.shape[0] == rows:
        return x
    return jnp.concatenate([x, jnp.zeros((rows - x.shape[0],) + x.shape[1:], x.dtype)], axis=0)


def _norm_matmul_kernel(x_ref, g_ref, w_ref, o_ref):
    h = _rms(x_ref[...], g_ref[...]).astype(BF16)
    o_ref[...] = _dg(h, w_ref[...])


def norm_matmul(x, g, w):
    n, d = x.shape
    c = w.shape[1]
    tm = _pick_tm(n)
    return pl.pallas_call(
        _norm_matmul_kernel,
        grid=(n // tm,),
        in_specs=[pl.BlockSpec((tm, d), lambda i: (i, 0)),
                  pl.BlockSpec((1, d), lambda i: (0, 0)),
                  pl.BlockSpec((d, c), lambda i: (0, 0))],
        out_specs=pl.BlockSpec((tm, c), lambda i: (i, 0)),
        out_shape=jax.ShapeDtypeStruct((n, c), F32),
        compiler_params=_cparams(("parallel",)),
        name="norm_matmul",
    )(x, g, w)


def _proj_res_kernel(*refs, n_in):
    ins, ws = refs[:n_in], refs[n_in:2 * n_in]
    g_ref, x_ref, o_ref = refs[2 * n_in:]
    acc = _dg(ins[0][...].astype(BF16), ws[0][...])
    for a, w in zip(ins[1:], ws[1:]):
        acc = acc + _dg(a[...].astype(BF16), w[...])
    o_ref[...] = x_ref[...] + _rms(acc, g_ref[...])


def proj_res(ins, ws, g, x):
    n, d = x.shape
    tm = _pick_tm(n)
    n_in = len(ins)
    in_specs = ([pl.BlockSpec((tm, a.shape[1]), lambda i: (i, 0)) for a in ins]
                + [pl.BlockSpec(w.shape, lambda i: (0, 0)) for w in ws]
                + [pl.BlockSpec((1, d), lambda i: (0, 0)), pl.BlockSpec((tm, d), lambda i: (i, 0))])
    return pl.pallas_call(
        functools.partial(_proj_res_kernel, n_in=n_in),
        grid=(n // tm,),
        in_specs=in_specs,
        out_specs=pl.BlockSpec((tm, d), lambda i: (i, 0)),
        out_shape=jax.ShapeDtypeStruct((n, d), F32),
        compiler_params=_cparams(("parallel",)),
        name="proj_res",
    )(*ins, *ws, g, x)


def _ffn_kernel(x_ref, gpre_ref, wg_ref, wu_ref, wd_ref, gpost_ref, o_ref, *, n_split):
    x = x_ref[...]
    h = _rms(x, gpre_ref[...]).astype(BF16)
    d_ff = wg_ref.shape[1]
    step = d_ff // n_split
    ff = None
    for s in range(n_split):
        gate = _dg(h, wg_ref[:, s * step:(s + 1) * step])
        up = _dg(h, wu_ref[:, s * step:(s + 1) * step])
        a = (gate * jax.nn.sigmoid(gate) * up).astype(BF16)
        part = _dg(a, wd_ref[s * step:(s + 1) * step, :])
        ff = part if ff is None else ff + part
    o_ref[...] = x + _rms(ff, gpost_ref[...])


def ffn(x, gpre, wg, wu, wd, gpost):
    n, d = x.shape
    d_ff = wg.shape[1]
    tm = _pick_tm(n, 256)
    n_split = 2 if d_ff % (2 * LANE) == 0 else 1
    const = lambda i: (0, 0)
    return pl.pallas_call(
        functools.partial(_ffn_kernel, n_split=n_split),
        grid=(n // tm,),
        in_specs=[pl.BlockSpec((tm, d), lambda i: (i, 0)),
                  pl.BlockSpec((1, d), const),
                  pl.BlockSpec((d, d_ff), const),
                  pl.BlockSpec((d, d_ff), const),
                  pl.BlockSpec((d_ff, d), const),
                  pl.BlockSpec((1, d), const)],
        out_specs=pl.BlockSpec((tm, d), lambda i: (i, 0)),
        out_shape=jax.ShapeDtypeStruct((n, d), F32),
        compiler_params=_cparams(("parallel",)),
        name="ffn",
    )(x, gpre, wg, wu, wd, gpost)


def _later_matrix():
    j = lax.broadcasted_iota(jnp.int32, (CHUNK, 2 * CHUNK), 0)
    s = lax.broadcasted_iota(jnp.int32, (CHUNK, 2 * CHUNK), 1)
    return jnp.where((s >= CHUNK) | (j > s), 1.0, 0.0).astype(BF16)


def _sb_scores(z, later, mask):
    sp = jnp.maximum(z, 0.0) + jnp.log(1.0 + jnp.exp(-jnp.abs(z)))
    lb = z - sp
    if mask is not None:
        sp = jnp.where(mask, sp, 0.0)
    return lb, _dg(sp.astype(BF16), later)


def _sb_prompt_kernel(bias_ref, q_ref, k_ref, v_ref, o_ref, acc_ref, ra_ref, rb_ref, *, first_valid, scale):
    hp = pl.program_id(1)
    qi = pl.program_id(2)
    tq = q_ref.shape[1]
    q = q_ref[0] * scale
    lane = lax.broadcasted_iota(jnp.int32, (1, LANE), 1)
    head_a = lane < DH_A
    qa = jnp.where(head_a, q, 0.0).astype(BF16)
    qb = jnp.where(head_a, 0.0, q).astype(BF16)
    bias_a = bias_ref[2 * hp]
    bias_b = bias_ref[2 * hp + 1]
    later = _later_matrix()
    acc_ref[...] = jnp.zeros_like(acc_ref)
    ra_ref[...] = jnp.zeros_like(ra_ref)
    rb_ref[...] = jnp.zeros_like(rb_ref)

    n_grp = tq // CHUNK

    def process_group(j0, masked):
        tiles = []
        for g in reversed(range(n_grp)):
            kj = j0 + g
            start = pl.multiple_of(kj * CHUNK, CHUNK)
            k = k_ref[0, pl.ds(start, CHUNK), :].astype(BF16)
            krow = kj * CHUNK + lax.broadcasted_iota(jnp.int32, (CHUNK, 1), 0)
            v = jnp.where(krow >= first_valid, v_ref[0, pl.ds(start, CHUNK), :], 0.0).astype(BF16)
            mask = None
            if masked:
                t_pos = qi * tq + lax.broadcasted_iota(jnp.int32, (tq, CHUNK), 0)
                s_pos = kj * CHUNK + lax.broadcasted_iota(jnp.int32, (tq, CHUNK), 1)
                mask = s_pos < t_pos
            sc_a = _sb_scores(_dg(qa, k, NT) + bias_a, later, mask)
            sc_b = _sb_scores(_dg(qb, k, NT) + bias_b, later, mask)
            tiles.append((sc_a, sc_b, v, mask))
        ra, rb, acc = ra_ref[...], rb_ref[...], acc_ref[...]
        for (lb_a, cs_a), (lb_b, cs_b), v, mask in tiles:
            att_a = jnp.exp(lb_a - cs_a[:, :CHUNK] - ra)
            att_b = jnp.exp(lb_b - cs_b[:, :CHUNK] - rb)
            if mask is not None:
                att_a = jnp.where(mask, att_a, 0.0)
                att_b = jnp.where(mask, att_b, 0.0)
            acc = acc + jnp.where(head_a, _dg(att_a.astype(BF16), v), _dg(att_b.astype(BF16), v))
            ra = ra + cs_a[:, CHUNK:]
            rb = rb + cs_b[:, CHUNK:]
        ra_ref[...], rb_ref[...], acc_ref[...] = ra, rb, acc

    process_group(qi * n_grp, True)

    def body(i, carry):
        process_group((qi - 1 - i) * n_grp, False)
        return carry

    lax.fori_loop(0, qi, body, 0)
    o_ref[0] = acc_ref[...]


def sb_prompt(proj, bias, first_valid):
    b, t, _ = proj.shape
    tq = 3 * CHUNK if t % (3 * CHUNK) == 0 else CHUNK
    n_hp = H_A // 2
    kern = functools.partial(_sb_prompt_kernel, first_valid=first_valid, scale=DH_A ** -0.5)
    return pl.pallas_call(
        kern,
        grid=(b, n_hp, t // tq),
        in_specs=[pl.BlockSpec(memory_space=pltpu.SMEM),
                  pl.BlockSpec((1, tq, LANE), lambda bi, h, qi: (bi, qi, h)),
                  pl.BlockSpec((1, t, LANE), lambda bi, h, qi: (bi, 0, n_hp + h)),
                  pl.BlockSpec((1, t, LANE), lambda bi, h, qi: (bi, 0, 2 * n_hp + h))],
        out_specs=pl.BlockSpec((1, tq, LANE), lambda bi, h, qi: (bi, qi, h)),
        out_shape=jax.ShapeDtypeStruct((b, t, W_A), F32),
        scratch_shapes=[pltpu.VMEM((tq, LANE), F32)] * 3,
        compiler_params=_cparams(("parallel", "parallel", "arbitrary")),
        name="sb_prompt",
    )(bias, proj, proj, proj)


def _sb_sample_kernel(pt_ref, q_ref, kn_ref, vn_ref, bias_ref, *rest, scale, n_g):
    kp_refs, vp_refs = rest[:n_g], rest[n_g:2 * n_g]
    o_ref, qbd_ref, acc_ref, r_ref = rest[2 * n_g:]
    p = pl.program_id(1)
    n_steps = pl.num_programs(1)
    tq = q_ref.shape[1]
    rows = H_A * tq
    later = _later_matrix()
    col_head = lax.broadcasted_iota(jnp.int32, (1, W_A), 1) >> 6

    def scores(z, mask):
        return _sb_scores(z + bias_ref[...], later, mask)

    @pl.when(p == 0)
    def _():
        q = q_ref[0] * scale
        qbd_ref[...] = jnp.concatenate(
            [jnp.where(col_head == h, q, 0.0) for h in range(H_A)], axis=0).astype(BF16)
        kn = _pad_rows(kn_ref[0], CHUNK).astype(BF16)
        vn = _pad_rows(vn_ref[0], CHUNK).astype(BF16)
        t_pos = lax.broadcasted_iota(jnp.int32, (rows, CHUNK), 0) & (tq - 1)
        s_pos = lax.broadcasted_iota(jnp.int32, (rows, CHUNK), 1)
        mask = s_pos < t_pos
        lb, cs = scores(_dg(qbd_ref[...], kn, NT), mask)
        att = jnp.where(mask, jnp.exp(lb - cs[:, :CHUNK]), 0.0)
        acc_ref[...] = _dg(att.astype(BF16), vn)
        r_ref[...] = cs[:, CHUNK:]

    qbd = qbd_ref[...]
    parts = [scores(_dg(qbd, kp_refs[j][0, 0].astype(BF16)), None) for j in range(n_g)]
    r = r_ref[...]
    acc = acc_ref[...]
    for j in range(n_g):
        lb, cs = parts[j]
        att = jnp.exp(lb - cs[:, :CHUNK] - r)
        acc = acc + _dg(att.astype(BF16), vp_refs[j][0, 0].astype(BF16), NT)
        r = r + cs[:, CHUNK:]
    acc_ref[...] = acc
    r_ref[...] = r

    @pl.when(p == n_steps - 1)
    def _():
        out = jnp.where(col_head == 0, acc[0:tq], 0.0)
        for h in range(1, H_A):
            out = out + jnp.where(col_head == h, acc[h * tq:(h + 1) * tq], 0.0)
        o_ref[0] = out


def sb_sample(proj, pool_k, pool_v, layer, page_table, bias):
    db, tq, _ = proj.shape
    n_pages = page_table.shape[1]
    page = pool_k.shape[3]
    assert page == CHUNK and tq % SUB == 0 and tq <= CHUNK and tq & (tq - 1) == 0 and DH_A == 64 and DH_B == 64
    n_g = 4 if n_pages % 4 == 0 else (2 if n_pages % 2 == 0 else 1)
    rows = H_A * tq
    bias_rows = jnp.broadcast_to(jnp.repeat(bias, tq)[:, None], (rows, CHUNK)).astype(F32)
    pt = page_table.reshape(-1).astype(jnp.int32)

    def page_spec(j):
        return pl.BlockSpec((1, 1, W_A, page), lambda bi, p, pt_ref: (
            layer, pt_ref[bi * n_pages + (n_pages - 1 - (p * n_g + j))], 0, 0))

    grid_spec = pltpu.PrefetchScalarGridSpec(
        num_scalar_prefetch=1,
        grid=(db, n_pages // n_g),
        in_specs=([pl.BlockSpec((1, tq, W_A), lambda bi, p, pt_ref: (bi, 0, 0)),
                   pl.BlockSpec((1, tq, W_A), lambda bi, p, pt_ref: (bi, 0, 1)),
                   pl.BlockSpec((1, tq, W_A), lambda bi, p, pt_ref: (bi, 0, 2)),
                   pl.BlockSpec((rows, CHUNK), lambda bi, p, pt_ref: (0, 0))]
                  + [page_spec(j) for j in range(n_g)] * 2),
        out_specs=pl.BlockSpec((1, tq, W_A), lambda bi, p, pt_ref: (bi, 0, 0)),
        scratch_shapes=[pltpu.VMEM((rows, W_A), BF16),
                        pltpu.VMEM((rows, W_A), F32),
                        pltpu.VMEM((rows, CHUNK), F32)],
    )
    return pl.pallas_call(
        functools.partial(_sb_sample_kernel, scale=DH_A ** -0.5, n_g=n_g),
        grid_spec=grid_spec,
        out_shape=jax.ShapeDtypeStruct((db, tq, W_A), F32),
        compiler_params=_cparams(("parallel", "arbitrary")),
        name="sb_sample",
    )(pt, proj, proj, proj, bias_rows, *([pool_k] * n_g), *([pool_v] * n_g))


def _run_interleaved(gens):
    results = [None] * len(gens)
    active = list(range(len(gens)))
    while active:
        for i in list(active):
            try:
                next(gens[i])
            except StopIteration as stop:
                results[i] = stop.value
                active.remove(i)
    return results


def _unit_lower_inverse_steps(n_mats, row, col, seq_log2):
    eye = jnp.where(row == col, 1.0, 0.0)
    log_base = min(4, seq_log2)
    ds = [jnp.where((row >> log_base) == (col >> log_base), n, 0.0) for n in n_mats]
    xs = [eye - d for d in ds]
    for _ in range(log_base - 1):
        ds = [_mm(d, d) for d in ds]
        yield
        xs = [x + _mm(x, d) for x, d in zip(xs, ds)]
        yield
    for lg in range(log_base, seq_log2):
        off = (row >> (lg + 1)) == (col >> (lg + 1))
        off = jnp.where(off, (row >> lg) & 1, 0) > ((col >> lg) & 1)
        ts = [_mm(jnp.where(off, n, 0.0), x) for n, x in zip(n_mats, xs)]
        yield
        xs = [x - _mm(x, t) for x, t in zip(xs, ts)]
        yield
    return xs


def _rwkv_pair_steps(r, k, v, lw, a, g, prm, valid, states, seq_log2):
    k_k, k_a, r_k, ln_w, ln_b = prm
    seq_len = 1 << seq_log2
    n_seq = CHUNK >> seq_log2
    row = lax.broadcasted_iota(jnp.int32, (CHUNK, CHUNK), 0)
    col = lax.broadcasted_iota(jnp.int32, (CHUNK, CHUNK), 1)
    head_a = lax.broadcasted_iota(jnp.int32, (1, LANE), 1) < DH_B
    same_head = (row >> 6) == (col >> 6)
    same_seq = (row >> seq_log2) == (col >> seq_log2)
    seg_sum = jnp.where(same_head, 1.0, 0.0).astype(BF16)
    seg_avg = jnp.where(same_head, 1.0 / DH_B, 0.0).astype(BF16)

    kk = k * k_k
    kk = kk / jnp.maximum(jnp.sqrt(_mm2l(kk * kk, seg_sum)), 1e-12)
    k_mod = k * (1.0 + (a - 1.0) * k_a)
    coef = _mm2l(r * k_mod * r_k, seg_sum)
    k_mod = jnp.where(valid, k_mod, 0.0)
    b = jnp.where(valid, kk * a, 0.0)

    incl = jnp.where(same_seq & (col <= row), 1.0, 0.0).astype(BF16)
    lw_hi = lw.astype(BF16)
    lw_mid = lw - lw_hi.astype(F32)
    lw_mid_b = lw_mid.astype(BF16)
    lw_lo = (lw_mid - lw_mid_b.astype(F32)).astype(BF16)
    cum = _dg(incl, lw_hi) + (_dg(incl, lw_mid_b) + _dg(incl, lw_lo))
    yield
    ends =[cum[(s + 1) * seq_len - 1:(s + 1) * seq_len, :] for s in range(n_seq)]
    cum_end = ends[0] if n_seq == 1 else jnp.concatenate(
        [jnp.broadcast_to(e, (seq_len, LANE)) for e in ends], axis=0)
    a_t = kk * jnp.exp(cum - lw)
    r_t = r * jnp.exp(cum)
    inv_p = jnp.exp(-cum)
    b_t = b * inv_p
    k_t = k_mod * inv_p
    to_end = jnp.exp(cum_end - cum)
    k_end = k_mod * to_end
    b_end = b * to_end
    cum_t = cum.T

    strict = same_seq & (col < row)
    incl_m = same_seq & (col <= row)
    lhs2 = jnp.concatenate([a_t, r_t], axis=0)
    n_mats, a_aks, a_rbs, a_rks = [], [], [], []
    for head_mask in (head_a, jnp.logical_not(head_a)):
        l2 = jnp.where(head_mask, lhs2, 0.0)
        m_b = _mm_tok(l2, b_t, NT)
        m_k = _mm_tok(l2, k_t, NT)
        n_mats.append(jnp.where(strict, m_b[:CHUNK], 0.0))
        a_aks.append(jnp.where(strict, m_k[:CHUNK], 0.0))
        a_rbs.append(jnp.where(incl_m, m_b[CHUNK:], 0.0))
        a_rks.append(jnp.where(incl_m, m_k[CHUNK:], 0.0))
    yield
    t_invs = yield from _unit_lower_inverse_steps(n_mats, row, col, seq_log2)
    parts = list(zip(t_invs, a_aks, a_rbs, a_rks))

    def sel(xa, xb):
        return jnp.where(head_a, xa, xb)

    if n_seq == 1:
        st = _mm_state(lhs2, states[0])
        a_h, r_h = st[:CHUNK], st[CHUNK:]
    else:
        a_hs, r_hs = [], []
        for s in range(n_seq):
            sl = slice(s * seq_len, (s + 1) * seq_len)
            st = _mm_state(jnp.concatenate([a_t[sl], r_t[sl]], axis=0), states[s])
            a_hs.append(st[:seq_len])
            r_hs.append(st[seq_len:])
        a_h = jnp.concatenate(a_hs, axis=0)
        r_h = jnp.concatenate(r_hs, axis=0)

    rhs = a_h + sel(_mm_tok(parts[0][1], v), _mm_tok(parts[1][1], v))
    yield
    u = sel(_mm_tok(parts[0][0], rhs), _mm_tok(parts[1][0], rhs))
    yield
    vu = jnp.concatenate([v, u], axis=0)
    y = r_h + sel(
        _mm_tok(jnp.concatenate([parts[0][3], -parts[0][2]], axis=1), vu),
        _mm_tok(jnp.concatenate([parts[1][3], -parts[1][2]], axis=1), vu))

    new_states = []
    for s in range(n_seq):
        sl = slice(s * seq_len, (s + 1) * seq_len)
        upd = _mm_state(jnp.concatenate([k_end[sl], -b_end[sl]], axis=0),
                        jnp.concatenate([v[sl], u[sl]], axis=0), TN)
        p_end_col = jnp.exp(cum_t[:, (s + 1) * seq_len - 1:(s + 1) * seq_len])
        new_states.append(states[s] * p_end_col + jnp.where(same_head, upd, 0.0))
    yield

    mean = _mm2l(y, seg_avg)
    yc = y - mean
    yield
    var = _mm2l(yc * yc, seg_avg)
    yn = yc * lax.rsqrt(var + GN_EPS) * ln_w + ln_b
    return (yn + coef * v) * g, new_states


def _rwkv_kernel(r_ref, k_ref, v_ref, da_ref, g_ref, sr_ref, sk_ref, sv_ref, sda_ref, sg_ref,
                 hb0_ref, p_ref, mu2_ref, wd_ref, wa_ref, wg_ref, o_ref, hb_out_ref, *scratch,
                 n_pairs, seq_log2, first_valid):
    stacked = seq_log2 < 7
    rcol = lax.broadcasted_iota(jnp.int32, (CHUNK, 1), 0)
    if stacked:
        valid = rcol >= 0
        seq_start = (rcol & ((1 << seq_log2) - 1)) == 0
    else:
        hb_ref, carry_ref, carry2_ref = scratch
        c = pl.program_id(1)
        n_c = pl.num_programs(1)
        rglob = c * CHUNK + rcol
        valid = rglob >= first_valid

        @pl.when(c == 0)
        def _():
            hb_ref[...] = hb0_ref[0]
            for j, s_ref in enumerate((sr_ref, sk_ref, sv_ref)):
                carry_ref[j] = jnp.broadcast_to(s_ref[0, 0:1, :], carry_ref.shape[1:])
            for j, s_ref in enumerate((sda_ref, sg_ref)):
                carry2_ref[j] = jnp.broadcast_to(s_ref[0, 0:1, :], carry2_ref.shape[1:])

    P = p_ref[...]
    mu2 = mu2_ref[...]
    mus = (P[0:1], P[1:2], P[2:3], mu2[0:1], mu2[1:2])
    x_refs = (r_ref, k_ref, v_ref, da_ref, g_ref)
    s_refs = (sr_ref, sk_ref, sv_ref, sda_ref, sg_ref)
    us = []
    for j in range(5):
        x = x_refs[j][...]
        rolled = pltpu.roll(x, 1, axis=0)
        if stacked:
            prev = jnp.where(seq_start, s_refs[j][...], rolled)
        else:
            cref, jj = (carry_ref, j) if j < 3 else (carry2_ref, j - 3)
            prev = jnp.where(rcol == 0, cref[jj][SUB - 1:SUB, :], rolled)
            prev = jnp.where(rglob == first_valid, s_refs[j][0, 0:1, :], prev)
            cref[jj] = x[CHUNK - SUB:CHUNK, :]
        us.append(x + (prev - x) * mus[j])
    r, k, v, u_da, u_g = us

    xw = P[3:4] + _mm(jnp.tanh(u_da), wd_ref[...])
    w_log = -_softplus(-xw) - 0.5
    lw = jnp.where(valid, -jnp.exp(w_log), 0.0)
    a = jax.nn.sigmoid(P[4:5] + _mm(u_da, wa_ref[...]))
    g = _mm(jax.nn.sigmoid(u_g), wg_ref[...])

    gens = []
    for pr in range(n_pairs):
        sl = slice(pr * LANE, (pr + 1) * LANE)
        prm = tuple(P[i:i + 1, sl] for i in range(5, 10))
        if stacked:
            states = [hb0_ref[s, 0] for s in range(CHUNK >> seq_log2)]
        else:
            states = [hb_ref[pr]]
        gens.append(_rwkv_pair_steps(r[:, sl], k[:, sl], v[:, sl], lw[:, sl], a[:, sl], g[:, sl], prm,
                                     valid, states, seq_log2))
    for pr, (out, new_states) in enumerate(_run_interleaved(gens)):
        sl = slice(pr * LANE, (pr + 1) * LANE)
        o_ref[:, sl] = out
        if stacked:
            for s, st in enumerate(new_states):
                hb_out_ref[s, 0] = st
        else:
            hb_ref[pr] = new_states[0]

    if not stacked:
        @pl.when(c == n_c - 1)
        def _():
            hb_out_ref[0] = hb_ref[...]


def rwkv(proj, shift_rows, hb0, params, mu2, wd, wa, wg, seq_len, first_valid):
    n, _ = proj.shape
    n_pr = H_B // 2
    stacked = seq_len < CHUNK
    if stacked:
        assert CHUNK % seq_len == 0 and n % CHUNK == 0 and seq_len & (seq_len - 1) == 0
        seq_log2 = seq_len.bit_length() - 1
        n_pairs, per_step = 1, CHUNK // seq_len
        grid = (n // CHUNK, n_pr)
        rows = lambda g, pr: g
        pcol = lambda g, pr: pr
        hb_block = (per_step, 1, CHUNK, CHUNK)
        hb_map = lambda g, pr: (g, pr, 0, 0)
        s_rows = CHUNK
        srow = lambda g, pr: (g,)
        scratch = []
    else:
        assert seq_len % CHUNK == 0 and n % seq_len == 0
        seq_log2 = 7
        n_c = seq_len // CHUNK
        n_pairs = n_pr
        grid = (n // seq_len, n_c)
        rows = lambda bi, c: bi * n_c + c
        pcol = lambda bi, c: 0
        hb_block = (1, n_pr, CHUNK, CHUNK)
        hb_map = lambda bi, c: (bi, 0, 0, 0)
        s_rows = SUB
        srow = lambda bi, c: (bi, 0)
        scratch = [pltpu.VMEM((n_pr, CHUNK, CHUNK), F32), pltpu.VMEM((3, SUB, W_B), F32),
                   pltpu.VMEM((2, SUB, LANE), F32)]
    w = n_pairs * LANE
    q0 = 3 * W_A

    def x_spec(col0, width, per_pair):
        blk0 = col0 // width
        if per_pair:
            return pl.BlockSpec((CHUNK, width), lambda i, j: (rows(i, j), blk0 + pcol(i, j)))
        return pl.BlockSpec((CHUNK, width), lambda i, j: (rows(i, j), blk0))

    def s_spec(col0, width, per_pair):
        blk0 = col0 // width
        shape = (s_rows, width) if stacked else (1, s_rows, width)
        if per_pair:
            return pl.BlockSpec(shape, lambda i, j: srow(i, j) + (blk0 + pcol(i, j),))
        return pl.BlockSpec(shape, lambda i, j: srow(i, j) + (blk0,))

    cols = [(0, w, True), (W_B, w, True), (2 * W_B, w, True), (3 * W_B, LANE, False), (3 * W_B + LANE, LANE, False)]
    in_specs = ([x_spec(q0 + c0, wd_, pp) for c0, wd_, pp in cols]
                + [s_spec(c0, wd_, pp) for c0, wd_, pp in cols]
                + [pl.BlockSpec(hb_block, hb_map),
                   pl.BlockSpec((16, w), lambda i, j: (0, pcol(i, j))),
                   pl.BlockSpec((SUB, LANE), lambda i, j: (0, 0)),
                   pl.BlockSpec((LANE, w), lambda i, j: (0, pcol(i, j))),
                   pl.BlockSpec((LANE, w), lambda i, j: (0, pcol(i, j))),
                   pl.BlockSpec((LANE, w), lambda i, j: (0, pcol(i, j)))])
    kern = functools.partial(_rwkv_kernel, n_pairs=n_pairs, seq_log2=seq_log2, first_valid=first_valid)
    return pl.pallas_call(
        kern,
        grid=grid,
        in_specs=in_specs,
        out_specs=[pl.BlockSpec((CHUNK, w), lambda i, j: (rows(i, j), pcol(i, j))),
                   pl.BlockSpec(hb_block, hb_map)],
        out_shape=[jax.ShapeDtypeStruct((n, W_B), F32),
                   jax.ShapeDtypeStruct(hb0.shape, F32)],
        scratch_shapes=scratch,
        compiler_params=_cparams(("parallel", "arbitrary")),
        name="rwkv",
    )(proj, proj, proj, proj, proj, shift_rows, shift_rows, shift_rows, shift_rows, shift_rows,
      hb0, params, mu2, wd, wa, wg)


def _scan_rows(x, n):
    ridx = lax.broadcasted_iota(jnp.int32, (n, 1), 0)
    s = 1
    while s < n:
        x = x + jnp.where(ridx >= s, pltpu.roll(x, s, axis=0), 0.0)
        s *= 2
    return x


def _scan_lanes(x):
    lidx = lax.broadcasted_iota(jnp.int32, (1, LANE), 1)
    s = 1
    while s < LANE:
        x = x + jnp.where(lidx >= s, pltpu.roll(x, s, axis=1), 0.0)
        s *= 2
    return x


def _mlstm_kernel(q_ref, k_ref, v_ref, og_ref, gc_ref, gt_ref, conv0_ref, c0_ref, n0_ref, m0_ref,
                  cw_ref, cb_ref, gb_row_ref, gb_col_ref, nw_ref,
                  o_ref, c_out_ref, n_out_ref, m_out_ref,
                  cext_ref, m_ref, halo_ref, *, first_valid):
    c = pl.program_id(1)
    n_c = pl.num_programs(1)
    lq = q_ref.shape[1]
    rcol = lax.broadcasted_iota(jnp.int32, (lq, 1), 0)
    rglob = c * lq + rcol
    valid_c = rglob >= first_valid
    lrow = lax.broadcasted_iota(jnp.int32, (1, LANE), 1)
    valid_r = ((c * lq + lrow) >= first_valid) & (lrow < lq)
    causal = lax.broadcasted_iota(jnp.int32, (lq, LANE), 1) <= lax.broadcasted_iota(jnp.int32, (lq, LANE), 0)

    @pl.when(c == 0)
    def _():
        for h in range(H_C):
            cext_ref[h, :, :DV_C] = c0_ref[0, h]
            cext_ref[h, :, DV_C:] = n0_ref[0, h]
        m_ref[...] = m0_ref[0]
        halo_ref[...] = conv0_ref[0]

    x = jnp.where(valid_c, jnp.concatenate([q_ref[0], k_ref[0]], axis=1), 0.0)
    xcat = jnp.concatenate([halo_ref[...], x], axis=0)
    cw = cw_ref[...]
    conv = cb_ref[...] + x * cw[CONV_W - 1:CONV_W]
    for dlt in range(1, CONV_W):
        conv = conv + pltpu.roll(xcat, dlt, axis=0)[SUB:SUB + lq] * cw[CONV_W - 1 - dlt:CONV_W - dlt]
    halo_ref[...] = xcat[lq:lq + SUB]
    qk = conv * jax.nn.sigmoid(conv)

    gcol = gc_ref[0] + gb_row_ref[...]
    lane_is_f = (lrow >= H_C) & (lrow < 2 * H_C)
    gcol = jnp.where(lane_is_f, -_softplus(-gcol), gcol)
    gcol = jnp.where(valid_c, gcol, jnp.where(lane_is_f, 0.0, -jnp.inf))
    bcum_cols = _scan_rows(jnp.where(lane_is_f, gcol, 0.0), lq)
    grow = gt_ref[0] + gb_col_ref[...]
    srow = lax.broadcasted_iota(jnp.int32, (SUB, 1), 0)
    row_is_f = srow >= H_C
    grow = jnp.where(row_is_f, -_softplus(-grow), grow)
    grow = jnp.where(valid_r, grow, jnp.where(row_is_f, 0.0, -jnp.inf))
    bcum_rows = _scan_lanes(jnp.where(row_is_f, grow, 0.0))

    ones_ext = jnp.ones((lq, LANE), F32)

    def head_steps(h):
        q = qk[:, h * DK_C:(h + 1) * DK_C]
        k = qk[:, QK_C + h * DK_C:QK_C + (h + 1) * DK_C] * (DK_C ** -0.5)
        v_ext = jnp.concatenate([v_ref[0, :, h * DV_C:(h + 1) * DV_C], ones_ext], axis=1)
        li_c = gcol[:, h:h + 1]
        bc_c = bcum_cols[:, H_C + h:H_C + h + 1]
        li_r = grow[h:h + 1, :]
        bc_r = bcum_rows[H_C + h:H_C + h + 1, :]
        m_prev = m_ref[h, 0:1, 0:1]
        cext = cext_ref[h]

        dmat = jnp.where(causal, bc_c - bc_r + li_r, -jnp.inf)
        inter = bc_c + m_prev
        m_row = jnp.maximum(inter, jnp.max(dmat, axis=1, keepdims=True))
        k_pad = _pad_rows(k, CHUNK)
        s = _mm(q, k_pad, NT)
        q_c = _mm(q, cext)
        yield
        w_intra = jnp.exp(dmat - m_row)
        w_inter = jnp.exp(inter - m_row)
        s = s * w_intra
        yield
        num_ext = _mm(s, _pad_rows(v_ext, CHUNK)) + w_inter * q_c
        den = num_ext[:, DV_C:DV_C + 1]
        hh = num_ext[:, :DV_C] / jnp.maximum(jnp.abs(den), jnp.exp(-m_row))

        m_new = m_row[lq - 1:lq, :]
        bc_last = bc_c[lq - 1:lq, :]
        w_end = jnp.exp(bc_last - bc_c + li_c - m_new)
        dec = jnp.exp(bc_last + m_prev - m_new)
        cext_ref[h] = dec * cext + _mm(k_pad, _pad_rows(w_end * v_ext, CHUNK), TN)
        m_ref[h] = jnp.broadcast_to(m_new, (SUB, LANE))
        yield

        mu = jnp.mean(hh, axis=1, keepdims=True)
        hc = hh - mu
        yield
        var = jnp.mean(hc * hc, axis=1, keepdims=True)
        hn = hc * lax.rsqrt(var + RMS_EPS) * nw_ref[:, h * DV_C:(h + 1) * DV_C]
        o_ref[0, :, h * DV_C:(h + 1) * DV_C] = hn * jax.nn.sigmoid(og_ref[0, :, h * DV_C:(h + 1) * DV_C])

    _run_interleaved([head_steps(h) for h in range(H_C)])

    @pl.when(c == n_c - 1)
    def _():
        for h in range(H_C):
            c_out_ref[0, h] = cext_ref[h, :, :DV_C]
            n_out_ref[0, h] = cext_ref[h, :, DV_C:]
        m_out_ref[0] = m_ref[...]


def mlstm(proj, gates_t, conv0, c0, n0b, m0b, conv_w, conv_b, gb_row, gb_col, norm_w, first_valid):
    bsz, t, _ = proj.shape
    lq = min(t, CHUNK)
    n_c = t // lq
    assert t % lq == 0 and gates_t.shape[2] == n_c * CHUNK
    qk2 = 2 * QK_C
    kern = functools.partial(_mlstm_kernel, first_valid=first_valid)
    const2 = lambda bi, c: (0, 0)
    return pl.pallas_call(
        kern,
        grid=(bsz, n_c),
        in_specs=[pl.BlockSpec((1, lq, QK_C), lambda bi, c: (bi, c, 0)),
                  pl.BlockSpec((1, lq, QK_C), lambda bi, c: (bi, c, 1)),
                  pl.BlockSpec((1, lq, V_C), lambda bi, c: (bi, c, qk2 // V_C)),
                  pl.BlockSpec((1, lq, V_C), lambda bi, c: (bi, c, qk2 // V_C + 1)),
                  pl.BlockSpec((1, lq, LANE), lambda bi, c: (bi, c, (qk2 + 2 * V_C) // LANE)),
                  pl.BlockSpec((1, SUB, LANE), lambda bi, c: (bi, 0, c)),
                  pl.BlockSpec((1, SUB, qk2), lambda bi, c: (bi, 0, 0)),
                  pl.BlockSpec((1, H_C, DK_C, DV_C), lambda bi, c: (bi, 0, 0, 0)),
                  pl.BlockSpec((1, H_C, DK_C, LANE), lambda bi, c: (bi, 0, 0, 0)),
                  pl.BlockSpec((1, H_C, SUB, LANE), lambda bi, c: (bi, 0, 0, 0)),
                  pl.BlockSpec((SUB, qk2), const2),
                  pl.BlockSpec((1, qk2), const2),
                  pl.BlockSpec((1, LANE), const2),
                  pl.BlockSpec((SUB, LANE), const2),
                  pl.BlockSpec((1, V_C), const2)],
        out_specs=[pl.BlockSpec((1, lq, V_C), lambda bi, c: (bi, c, 0)),
                   pl.BlockSpec((1, H_C, DK_C, DV_C), lambda bi, c: (bi, 0, 0, 0)),
                   pl.BlockSpec((1, H_C, DK_C, LANE), lambda bi, c: (bi, 0, 0, 0)),
                   pl.BlockSpec((1, H_C, SUB, LANE), lambda bi, c: (bi, 0, 0, 0))],
        out_shape=[jax.ShapeDtypeStruct((bsz, t, V_C), F32),
                   jax.ShapeDtypeStruct((bsz, H_C, DK_C, DV_C), F32),
                   jax.ShapeDtypeStruct((bsz, H_C, DK_C, LANE), F32),
                   jax.ShapeDtypeStruct((bsz, H_C, SUB, LANE), F32)],
        scratch_shapes=[pltpu.VMEM((H_C, DK_C, DV_C + LANE), F32),
                        pltpu.VMEM((H_C, SUB, LANE), F32),
                        pltpu.VMEM((SUB, qk2), F32)],
        compiler_params=_cparams(("parallel", "arbitrary")),
        name="mlstm",
    )(proj, proj, proj, proj, proj, gates_t, conv0, c0, n0b, m0b, conv_w, conv_b, gb_row, gb_col, norm_w)


def _pad_to(x, axis, size):
    pad = [(0, 0)] * x.ndim
    pad[axis] = (0, size - x.shape[axis])
    return jnp.pad(x, pad)


def _rwkv_params(i, rwkv_mu, rwkv_w0, rwkv_a0, rwkv_k_k, rwkv_k_a, rwkv_r_k, rwkv_ln_w, rwkv_ln_b,
                 w_decay_up, w_iclr_up, w_gate_up):
    mu = rwkv_mu[i]
    rows = [mu[0:W_B], mu[W_B:2 * W_B], mu[2 * W_B:3 * W_B], rwkv_w0[i], rwkv_a0[i], rwkv_k_k[i],
            rwkv_k_a[i], rwkv_r_k[i], rwkv_ln_w[i], rwkv_ln_b[i]]
    pp = _pad_to(jnp.stack(rows, axis=0), 0, 16)
    mu2 = _pad_to(mu[3 * W_B:].reshape(2, LANE), 0, SUB)
    wd = jnp.concatenate([w_decay_up[i], jnp.zeros((R_ICLR, W_B), F32)], axis=0).astype(BF16)
    wa = jnp.concatenate([jnp.zeros((R_DECAY, W_B), F32), w_iclr_up[i]], axis=0).astype(BF16)
    wg = w_gate_up[i].astype(BF16)
    return pp, mu2, wd, wa, wg


def _pair_states(wkv):
    bsz = wkv.shape[0]
    st = jnp.swapaxes(wkv, -1, -2).reshape(bsz, H_B // 2, 2, DH_B, DH_B)
    z = jnp.zeros_like(st[:, :, 0])
    top = jnp.concatenate([st[:, :, 0], z], axis=-1)
    bot = jnp.concatenate([z, st[:, :, 1]], axis=-1)
    return jnp.concatenate([top, bot], axis=-2)


def _unpair_states(hb):
    bsz = hb.shape[0]
    a = hb[:, :, :DH_B, :DH_B]
    b = hb[:, :, DH_B:, DH_B:]
    st = jnp.stack([a, b], axis=2).reshape(bsz, H_B, DH_B, DH_B)
    return jnp.swapaxes(st, -1, -2)


def kernel(x_prompt, x_sample, cache_k, cache_v, page_table, state_wkv, state_shift, state_mlstm_c, state_mlstm_n, state_mlstm_m, state_conv, meta_tokens, g_mix_pre, g_mix_post, g_ffn_pre, g_ffn_post, w_in_ab, w_out_ab, sb_bias, rwkv_mu, rwkv_w0, rwkv_w_decay_up, rwkv_a0, rwkv_w_iclr_up, rwkv_w_gate_up, rwkv_k_k, rwkv_k_a, rwkv_r_k, rwkv_ln_w, rwkv_ln_b, w_in_c, conv_w, conv_b, mlstm_b_i, mlstm_b_f, mlstm_norm_w, w_out_c, w_ffn_gate, w_ffn_up, w_ffn_down):
    bsz, seq, d = x_prompt.shape
    db, ds, _ = x_sample.shape
    depth = g_mix_pre.shape[0]
    tp = PAD_FRONT + N_META + seq
    assert seq % CHUNK == 0 and ds % SUB == 0 and ds <= CHUNK and seq >= CONV_W and ds >= CONV_W

    meta = jnp.broadcast_to(meta_tokens.astype(F32)[None], (bsz, N_META, d))
    xp = jnp.concatenate([jnp.zeros((bsz, PAD_FRONT, d), F32), meta, x_prompt], axis=1).reshape(bsz * tp, d)
    xs = x_sample.reshape(db * ds, d)

    n_layers_ab, n_pool = cache_k.shape[:2]
    pool_k = jnp.transpose(cache_k, (0, 1, 3, 4, 2)).reshape(n_layers_ab, n_pool, W_A, CHUNK)
    pool_v = jnp.transpose(cache_v, (0, 1, 3, 4, 2)).reshape(n_layers_ab, n_pool, W_A, CHUNK)
    outs = {name: [] for name in ("k_p", "v_p", "wkv_p", "sh_p", "c_p", "n_p", "m_p", "cv_p",
                                  "k_s", "v_s", "wkv_s", "sh_s", "c_s", "n_s", "m_s", "cv_s")}
    row = lambda v: v.reshape(1, -1)

    for layer in range(depth):
        i = layer // 2
        if layer % 2 == 0:
            w_in = w_in_ab[i].astype(BF16)
            w_out = w_out_ab[i].astype(BF16)
            pp, mu2, wd, wa, wg = _rwkv_params(i, rwkv_mu, rwkv_w0, rwkv_a0, rwkv_k_k, rwkv_k_a, rwkv_r_k,
                                               rwkv_ln_w, rwkv_ln_b, rwkv_w_decay_up, rwkv_w_iclr_up, rwkv_w_gate_up)
            new_x = []
            for grp, x in (("p", xp), ("s", xs)):
                nb, t = (bsz, tp) if grp == "p" else (db, ds)
                proj = norm_matmul(x, row(g_mix_pre[layer]), w_in).reshape(nb, t, AB_COLS)
                if grp == "p":
                    o_a = sb_prompt(proj, sb_bias[i], PAD_FRONT)
                    shift_rows = jnp.zeros((nb, SUB, B_COLS), F32)
                    hb0 = jnp.zeros((nb, H_B // 2, CHUNK, CHUNK), F32)
                    first_valid = PAD_FRONT
                else:
                    o_a = sb_sample(proj, pool_k, pool_v, i, page_table, sb_bias[i])
                    shift_rows = _pad_to(state_shift[i][:, None, :], 1, t).reshape(nb * t, B_COLS)
                    hb0 = _pair_states(state_wkv[i])
                    first_valid = 0
                o_b, hb = rwkv(proj.reshape(nb * t, AB_COLS), shift_rows, hb0, pp, mu2, wd, wa, wg, t, first_valid)
                new_x.append(proj_res([o_a.reshape(nb * t, W_A), o_b.reshape(nb * t, W_B)],
                                      [w_out[:W_A], w_out[W_A:]], row(g_mix_post[layer]), x))
                outs["k_" + grp].append(proj[:, first_valid:, W_A:2 * W_A].reshape(nb, t - first_valid, H_A, DH_A))
                outs["v_" + grp].append(proj[:, first_valid:, 2 * W_A:3 * W_A].reshape(nb, t - first_valid, H_A, DH_A))
                outs["wkv_" + grp].append(_unpair_states(hb))
                outs["sh_" + grp].append(proj[:, t - 1, 3 * W_A:])
            xp, xs = new_x
        else:
            wc = w_in_c[i]
            ncol = wc.shape[1]
            wperm = jnp.concatenate([wc[:, :2 * QK_C + V_C], wc[:, ncol - V_C:],
                                     wc[:, 2 * QK_C + V_C:ncol - V_C]], axis=1)
            wperm = _pad_to(wperm, 1, C_COLS_PAD).astype(BF16)
            w_out = w_out_c[i].astype(BF16)
            cw = _pad_to(conv_w[i], 0, SUB)
            gb = jnp.concatenate([mlstm_b_i[i], mlstm_b_f[i]])
            gb_row = _pad_to(gb, 0, LANE).reshape(1, LANE)
            gb_col = jnp.broadcast_to(gb[:, None], (SUB, LANE))
            new_x = []
            for grp, x in (("p", xp), ("s", xs)):
                nb, t = (bsz, tp) if grp == "p" else (db, ds)
                proj = norm_matmul(x, row(g_mix_pre[layer]), wperm).reshape(nb, t, C_COLS_PAD)
                g0 = 2 * QK_C + 2 * V_C
                gates_t = jnp.swapaxes(proj[:, :, g0:g0 + SUB], 1, 2)
                if grp == "p":
                    first_valid = PAD_FRONT
                    conv0 = jnp.zeros((nb, SUB, 2 * QK_C), F32)
                    c0 = jnp.zeros((nb, H_C, DK_C, DV_C), F32)
                    n0 = jnp.zeros((nb, H_C, DK_C), F32)
                    m0 = jnp.zeros((nb, H_C), F32)
                else:
                    first_valid = 0
                    gates_t = _pad_to(gates_t, 2, CHUNK)
                    conv0 = jnp.concatenate([jnp.zeros((nb, SUB - (CONV_W - 1), 2 * QK_C), F32), state_conv[i]], axis=1)
                    c0, n0, m0 = state_mlstm_c[i], state_mlstm_n[i], state_mlstm_m[i]
                n0b = jnp.broadcast_to(n0[..., None], (nb, H_C, DK_C, LANE))
                m0b = jnp.broadcast_to(m0[..., None, None], (nb, H_C, SUB, LANE))
                hmix, c_new, n_new, m_new = mlstm(proj, gates_t, conv0, c0, n0b, m0b, cw, row(conv_b[i]),
                                                  gb_row, gb_col, row(mlstm_norm_w[i]), first_valid)
                new_x.append(proj_res([hmix.reshape(nb * t, V_C)], [w_out], row(g_mix_post[layer]), x))
                outs["c_" + grp].append(c_new)
                outs["n_" + grp].append(n_new[..., 0])
                outs["m_" + grp].append(m_new[..., 0, 0])
                outs["cv_" + grp].append(proj[:, t - (CONV_W - 1):, :2 * QK_C])
            xp, xs = new_x
        wgt, wup, wdn = (w_ffn_gate[layer].astype(BF16), w_ffn_up[layer].astype(BF16),
                         w_ffn_down[layer].astype(BF16))
        xp = ffn(xp, row(g_ffn_pre[layer]), wgt, wup, wdn, row(g_ffn_post[layer]))
        xs = ffn(xs, row(g_ffn_pre[layer]), wgt, wup, wdn, row(g_ffn_post[layer]))

    y_prompt = xp.reshape(bsz, tp, d)[:, PAD_FRONT + N_META:]
    y_sample = xs.reshape(db, ds, d)
    st = lambda name: jnp.stack(outs[name])
    return (y_prompt, y_sample, st("k_p"), st("v_p"), st("wkv_p"), st("sh_p"), st("c_p"), st("n_p"),
            st("m_p"), st("cv_p"), st("k_s"), st("v_s"), st("wkv_s"), st("sh_s"), st("c_s"), st("n_s"),
            st("m_s"), st("cv_s"))
```

```python
import functools

import jax
import jax.numpy as jnp
from jax import lax
from jax.experimental import pallas as pl
from jax.experimental.pallas import tpu as pltpu

F32 = jnp.float32
BF16 = jnp.bfloat16

D_MODEL = 1024
N_META = 16
H_A, DH_A = 8, 64
W_A = H_A * DH_A
H_B, DH_B = 8, 64
W_B = H_B * DH_B
R_DECAY, R_ICLR, R_GATE = 64, 64, 128
B_COLS = 3 * W_B + R_DECAY + R_ICLR + R_GATE
AB_COLS = 3 * W_A + B_COLS
H_C, DK_C, DV_C = 4, 128, 256
QK_C = H_C * DK_C
V_C = H_C * DV_C
CONV_W = 4
C_COLS_PAD = 2 * QK_C + 2 * V_C + 128
RMS_EPS = 1e-6
GN_EPS = 64e-5
LOG2E = 1.4426950408889634

LANE = 128
SUB = 8
CHUNK = 128
PAD_FRONT = CHUNK - N_META
VMEM_LIMIT = 56 * 1024 * 1024

NN = (((1,), (0,)), ((), ()))
NT = (((1,), (1,)), ((), ()))
TN = (((0,), (0,)), ((), ()))


def _cparams(sem):
    return pltpu.CompilerParams(dimension_semantics=sem, vmem_limit_bytes=VMEM_LIMIT)


def _dg(a, b, dims=NN):
    return lax.dot_general(a, b, dims, preferred_element_type=F32)


def _split(x):
    hi = x.astype(BF16)
    lo = (x - hi.astype(F32)).astype(BF16)
    return hi, lo


def _mm(a, b, dims=NN):
    return _dg(a.astype(BF16), b.astype(BF16), dims)


def _mm3(a, b, dims=NN):
    ah, al = _split(a)
    bh, bl = _split(b)
    return _dg(ah, bh, dims) + (_dg(ah, bl, dims) + _dg(al, bh, dims))


def _mm2l(a, b_bf16, dims=NN):
    ah, al = _split(a)
    return _dg(ah, b_bf16, dims) + _dg(al, b_bf16, dims)


_mm_tok = _mm
_mm_state = _mm


def _rms(x, g):
    ms = jnp.mean(x * x, axis=-1, keepdims=True)
    return x * lax.rsqrt(ms + RMS_EPS) * g


def _softplus(z):
    return jnp.maximum(z, 0.0) + jnp.log1p(jnp.exp(-jnp.abs(z)))


def _pick_tm(n, cap=512):
    for tm in (512, 384, 256, 128, 64, 32, 16, 8):
        if tm <= cap and n % tm == 0:
            return tm
    raise ValueError(f"row count {n} not a multiple of 8")


def _pad_rows(x, rows):
    if x.shape[0] == rows:
        return x
    return jnp.concatenate([x, jnp.zeros((rows - x.shape[0],) + x.shape[1:], x.dtype)], axis=0)


def _norm_matmul_kernel(x_ref, g_ref, w_ref, o_ref):
    h = _rms(x_ref[...], g_ref[...]).astype(BF16)
    o_ref[...] = _dg(h, w_ref[...])


def norm_matmul(x, g, w):
    n, d = x.shape
    c = w.shape[1]
    tm = _pick_tm(n)
    return pl.pallas_call(
        _norm_matmul_kernel,
        grid=(n // tm,),
        in_specs=[pl.BlockSpec((tm, d), lambda i: (i, 0)),
                  pl.BlockSpec((1, d), lambda i: (0, 0)),
                  pl.BlockSpec((d, c), lambda i: (0, 0))],
        out_specs=pl.BlockSpec((tm, c), lambda i: (i, 0)),
        out_shape=jax.ShapeDtypeStruct((n, c), F32),
        compiler_params=_cparams(("parallel",)),
        name="norm_matmul",
    )(x, g, w)


def _proj_res_kernel(*refs, n_in):
    ins, ws = refs[:n_in], refs[n_in:2 * n_in]
    g_ref, x_ref, o_ref = refs[2 * n_in:]
    acc = _dg(ins[0][...].astype(BF16), ws[0][...])
    for a, w in zip(ins[1:], ws[1:]):
        acc = acc + _dg(a[...].astype(BF16), w[...])
    o_ref[...] = x_ref[...] + _rms(acc, g_ref[...])


def proj_res(ins, ws, g, x):
    n, d = x.shape
    tm = _pick_tm(n)
    n_in = len(ins)
    in_specs = ([pl.BlockSpec((tm, a.shape[1]), lambda i: (i, 0)) for a in ins]
                + [pl.BlockSpec(w.shape, lambda i: (0, 0)) for w in ws]
                + [pl.BlockSpec((1, d), lambda i: (0, 0)), pl.BlockSpec((tm, d), lambda i: (i, 0))])
    return pl.pallas_call(
        functools.partial(_proj_res_kernel, n_in=n_in),
        grid=(n // tm,),
        in_specs=in_specs,
        out_specs=pl.BlockSpec((tm, d), lambda i: (i, 0)),
        out_shape=jax.ShapeDtypeStruct((n, d), F32),
        compiler_params=_cparams(("parallel",)),
        name="proj_res",
    )(*ins, *ws, g, x)


def _ffn_kernel(x_ref, gpre_ref, wg_ref, wu_ref, wd_ref, gpost_ref, o_ref, *, n_split):
    x = x_ref[...]
    h = _rms(x, gpre_ref[...]).astype(BF16)
    d_ff = wg_ref.shape[1]
    step = d_ff // n_split
    ff = None
    for s in range(n_split):
        gate = _dg(h, wg_ref[:, s * step:(s + 1) * step])
        up = _dg(h, wu_ref[:, s * step:(s + 1) * step])
        a = (gate * jax.nn.sigmoid(gate) * up).astype(BF16)
        part = _dg(a, wd_ref[s * step:(s + 1) * step, :])
        ff = part if ff is None else ff + part
    o_ref[...] = x + _rms(ff, gpost_ref[...])


def ffn(x, gpre, wg, wu, wd, gpost):
    n, d = x.shape
    d_ff = wg.shape[1]
    tm = _pick_tm(n, 256)
    n_split = 2 if d_ff % (2 * LANE) == 0 else 1
    const = lambda i: (0, 0)
    return pl.pallas_call(
        functools.partial(_ffn_kernel, n_split=n_split),
        grid=(n // tm,),
        in_specs=[pl.BlockSpec((tm, d), lambda i: (i, 0)),
                  pl.BlockSpec((1, d), const),
                  pl.BlockSpec((d, d_ff), const),
                  pl.BlockSpec((d, d_ff), const),
                  pl.BlockSpec((d_ff, d), const),
                  pl.BlockSpec((1, d), const)],
        out_specs=pl.BlockSpec((tm, d), lambda i: (i, 0)),
        out_shape=jax.ShapeDtypeStruct((n, d), F32),
        compiler_params=_cparams(("parallel",)),
        name="ffn",
    )(x, gpre, wg, wu, wd, gpost)


def _later_strict(n_heads):
    n = n_heads * CHUNK
    j = lax.broadcasted_iota(jnp.int32, (n, n), 0)
    s = lax.broadcasted_iota(jnp.int32, (n, n), 1)
    return jnp.where(((j >> 7) == (s >> 7)) & (j > s), 1.0, 0.0).astype(BF16)


def _sb_terms2(z2, mask):
    neg_abs = lax.bitcast_convert_type(lax.bitcast_convert_type(z2, jnp.uint32) | jnp.uint32(0x80000000), F32)
    sp = jnp.maximum(z2, 0.0) + jnp.log2(1.0 + jnp.exp2(neg_abs))
    lb = z2 - sp
    if mask is not None:
        sp = jnp.where(mask, sp, 0.0)
    return lb, sp


def _sb_prompt_kernel(bias_ref, q_ref, k_ref, v_ref, o_ref, acc_ref, ra_ref, rb_ref, *, first_valid, scale):
    hp = pl.program_id(1)
    qi = pl.program_id(2)
    tq = q_ref.shape[1]
    q = (q_ref[0] * (scale * LOG2E)).astype(BF16)
    head_a = lax.broadcasted_iota(jnp.int32, (1, LANE), 1) < DH_A
    head_a2 = lax.broadcasted_iota(jnp.int32, (1, 2 * CHUNK), 1) < CHUNK
    bias2 = jnp.where(head_a2, bias_ref[2 * hp], bias_ref[2 * hp + 1]) * LOG2E
    later = _later_strict(2)
    acc_ref[...] = jnp.zeros_like(acc_ref)
    ra_ref[...] = jnp.zeros_like(ra_ref)
    rb_ref[...] = jnp.zeros_like(rb_ref)

    n_grp = tq // CHUNK

    def process_group(j0, masked):
        kjs = [j0 + g for g in reversed(range(n_grp))]
        starts = [pl.multiple_of(kj * CHUNK, CHUNK) for kj in kjs]
        masks = [None] * n_grp
        if masked:
            t_pos = qi * tq + lax.broadcasted_iota(jnp.int32, (tq, 2 * CHUNK), 0)
            s_loc = lax.broadcasted_iota(jnp.int32, (tq, 2 * CHUNK), 1) & (CHUNK - 1)
            masks = [kj * CHUNK + s_loc < t_pos for kj in kjs]
        z2s = []
        for start in starts:
            k = k_ref[0, pl.ds(start, CHUNK), :]
            k2 = jnp.concatenate([jnp.where(head_a, k, 0.0), jnp.where(head_a, 0.0, k)], axis=0).astype(BF16)
            z2s.append(_dg(q, k2, NT))
        tiles = []
        for z2, mask in zip(z2s, masks):
            lb, sp = _sb_terms2(z2 + bias2, mask)
            within = _dg(sp.astype(BF16), later)
            rs_a = jnp.sum(sp[:, :CHUNK], axis=1, keepdims=True)
            rs_b = jnp.sum(sp[:, CHUNK:], axis=1, keepdims=True)
            tiles.append((lb - within, rs_a, rs_b))
        ra, rb, acc = ra_ref[...], rb_ref[...], acc_ref[...]
        for (ex, rs_a, rs_b), mask, kj, start in zip(tiles, masks, kjs, starts):
            att = jnp.exp2(jnp.concatenate([ex[:, :CHUNK] - ra, ex[:, CHUNK:] - rb], axis=1))
            if mask is not None:
                att = jnp.where(mask, att, 0.0)
            krow = kj * CHUNK + lax.broadcasted_iota(jnp.int32, (CHUNK, 1), 0)
            v = jnp.where(krow >= first_valid, v_ref[0, pl.ds(start, CHUNK), :], 0.0)
            v2 = jnp.concatenate([jnp.where(head_a, v, 0.0), jnp.where(head_a, 0.0, v)], axis=0).astype(BF16)
            acc = acc + _dg(att.astype(BF16), v2)
            ra = ra + rs_a
            rb = rb + rs_b
        ra_ref[...], rb_ref[...], acc_ref[...] = ra, rb, acc

    process_group(qi * n_grp, True)

    def body(i, carry):
        process_group((qi - 1 - i) * n_grp, False)
        return carry

    lax.fori_loop(0, qi, body, 0)
    o_ref[0] = acc_ref[...]


def sb_prompt(proj, bias, first_valid):
    b, t, _ = proj.shape
    tq = 3 * CHUNK if t % (3 * CHUNK) == 0 else CHUNK
    n_hp = H_A // 2
    kern = functools.partial(_sb_prompt_kernel, first_valid=first_valid, scale=DH_A ** -0.5)
    return pl.pallas_call(
        kern,
        grid=(b, n_hp, t // tq),
        in_specs=[pl.BlockSpec(memory_space=pltpu.SMEM),
                  pl.BlockSpec((1, tq, LANE), lambda bi, h, qi: (bi, qi, h)),
                  pl.BlockSpec((1, t, LANE), lambda bi, h, qi: (bi, 0, n_hp + h)),
                  pl.BlockSpec((1, t, LANE), lambda bi, h, qi: (bi, 0, 2 * n_hp + h))],
        out_specs=pl.BlockSpec((1, tq, LANE), lambda bi, h, qi: (bi, qi, h)),
        out_shape=jax.ShapeDtypeStruct((b, t, W_A), F32),
        scratch_shapes=[pltpu.VMEM((tq, LANE), F32)] * 3,
        compiler_params=_cparams(("parallel", "parallel", "arbitrary")),
        name="sb_prompt",
    )(bias, proj, proj, proj)


def _sb_sample_kernel(pt_ref, q_ref, kn_ref, vn_ref, bias_ref, *rest, scale, n_g):
    kp_refs, vp_refs = rest[:n_g], rest[n_g:2 * n_g]
    o_ref, qbd_ref, acc_ref, r_ref = rest[2 * n_g:]
    p = pl.program_id(1)
    n_steps = pl.num_programs(1)
    tq = q_ref.shape[1]
    rows = H_A * tq
    later = _later_strict(1)
    col_head = lax.broadcasted_iota(jnp.int32, (1, W_A), 1) >> 6
    bias2 = bias_ref[...] * LOG2E

    def scores(z2s, mask):
        terms = [_sb_terms2(z2 + bias2, mask) for z2 in z2s]
        withins = [_dg(sp.astype(BF16), later) for _, sp in terms]
        return [(lb - within, jnp.sum(sp, axis=1, keepdims=True)) for (lb, sp), within in zip(terms, withins)]

    @pl.when(p == 0)
    def _():
        q = q_ref[0] * (scale * LOG2E)
        qbd_ref[...] = jnp.concatenate(
            [jnp.where(col_head == h, q, 0.0) for h in range(H_A)], axis=0).astype(BF16)
        kn = _pad_rows(kn_ref[0], CHUNK).astype(BF16)
        vn = _pad_rows(vn_ref[0], CHUNK).astype(BF16)
        t_pos = lax.broadcasted_iota(jnp.int32, (rows, CHUNK), 0) & (tq - 1)
        s_pos = lax.broadcasted_iota(jnp.int32, (rows, CHUNK), 1)
        mask = s_pos < t_pos
        (ex, rs), = scores([_dg(qbd_ref[...], kn, NT)], mask)
        att = jnp.where(mask, jnp.exp2(ex), 0.0)
        acc_ref[...] = _dg(att.astype(BF16), vn)
        r_ref[...] = jnp.broadcast_to(rs, r_ref.shape)

    qbd = qbd_ref[...]
    parts = scores([_dg(qbd, kp_refs[j][0, 0].astype(BF16)) for j in range(n_g)], None)
    r = r_ref[...]
    atts = []
    for ex, rs in parts:
        atts.append(jnp.exp2(ex - r).astype(BF16))
        r = r + rs
    acc = acc_ref[...]
    for j in range(n_g):
        acc = acc + _dg(atts[j], vp_refs[j][0, 0].astype(BF16), NT)
    acc_ref[...] = acc
    r_ref[...] = r

    @pl.when(p == n_steps - 1)
    def _():
        out = jnp.where(col_head == 0, acc[0:tq], 0.0)
        for h in range(1, H_A):
            out = out + jnp.where(col_head == h, acc[h * tq:(h + 1) * tq], 0.0)
        o_ref[0] = out


def sb_sample(proj, pool_k, pool_v, layer, page_table, bias):
    db, tq, _ = proj.shape
    n_pages = page_table.shape[1]
    page = pool_k.shape[3]
    assert page == CHUNK and tq % SUB == 0 and tq <= CHUNK and tq & (tq - 1) == 0 and DH_A == 64 and DH_B == 64
    n_g = next(g for g in (8, 4, 2, 1) if n_pages % g == 0)
    rows = H_A * tq
    bias_rows = jnp.broadcast_to(jnp.repeat(bias, tq)[:, None], (rows, CHUNK)).astype(F32)
    pt = page_table.reshape(-1).astype(jnp.int32)

    def page_spec(j):
        return pl.BlockSpec((1, 1, W_A, page), lambda bi, p, pt_ref: (
            layer, pt_ref[bi * n_pages + (n_pages - 1 - (p * n_g + j))], 0, 0))

    grid_spec = pltpu.PrefetchScalarGridSpec(
        num_scalar_prefetch=1,
        grid=(db, n_pages // n_g),
        in_specs=([pl.BlockSpec((1, tq, W_A), lambda bi, p, pt_ref: (bi, 0, 0)),
                   pl.BlockSpec((1, tq, W_A), lambda bi, p, pt_ref: (bi, 0, 1)),
                   pl.BlockSpec((1, tq, W_A), lambda bi, p, pt_ref: (bi, 0, 2)),
                   pl.BlockSpec((rows, CHUNK), lambda bi, p, pt_ref: (0, 0))]
                  + [page_spec(j) for j in range(n_g)] * 2),
        out_specs=pl.BlockSpec((1, tq, W_A), lambda bi, p, pt_ref: (bi, 0, 0)),
        scratch_shapes=[pltpu.VMEM((rows, W_A), BF16),
                        pltpu.VMEM((rows, W_A), F32),
                        pltpu.VMEM((rows, CHUNK), F32)],
    )
    return pl.pallas_call(
        functools.partial(_sb_sample_kernel, scale=DH_A ** -0.5, n_g=n_g),
        grid_spec=grid_spec,
        out_shape=jax.ShapeDtypeStruct((db, tq, W_A), F32),
        compiler_params=_cparams(("parallel", "arbitrary")),
        name="sb_sample",
    )(pt, proj, proj, proj, bias_rows, *([pool_k] * n_g), *([pool_v] * n_g))


def _run_interleaved(gens):
    results = [None] * len(gens)
    active = list(range(len(gens)))
    while active:
        for i in list(active):
            try:
                next(gens[i])
            except StopIteration as stop:
                results[i] = stop.value
                active.remove(i)
    return results


def _unit_lower_inverse_steps(n_mats, row, col, seq_log2):
    eye = jnp.where(row == col, 1.0, 0.0)
    log_base = min(4, seq_log2)
    ds = [jnp.where((row >> log_base) == (col >> log_base), n, 0.0) for n in n_mats]
    xs = [eye - d for d in ds]
    for _ in range(log_base - 1):
        ds = [_mm(d, d) for d in ds]
        yield
        xs = [x + _mm(x, d) for x, d in zip(xs, ds)]
        yield
    for lg in range(log_base, seq_log2):
        off = (row >> (lg + 1)) == (col >> (lg + 1))
        off = jnp.where(off, (row >> lg) & 1, 0) > ((col >> lg) & 1)
        ts = [_mm(jnp.where(off, n, 0.0), x) for n, x in zip(n_mats, xs)]
        yield
        xs = [x - _mm(x, t) for x, t in zip(xs, ts)]
        yield
    return xs


def _rwkv_pair_steps(r, k, v, lw, a, g, prm, valid, states, seq_log2):
    k_k, k_a, r_k, ln_w, ln_b = prm
    seq_len = 1 << seq_log2
    n_seq = CHUNK >> seq_log2
    row = lax.broadcasted_iota(jnp.int32, (CHUNK, CHUNK), 0)
    col = lax.broadcasted_iota(jnp.int32, (CHUNK, CHUNK), 1)
    head_a = lax.broadcasted_iota(jnp.int32, (1, LANE), 1) < DH_B
    same_head = (row >> 6) == (col >> 6)
    same_seq = (row >> seq_log2) == (col >> seq_log2)
    seg_sum = jnp.where(same_head, 1.0, 0.0).astype(BF16)
    seg_avg = jnp.where(same_head, 1.0 / DH_B, 0.0).astype(BF16)

    kk = k * k_k
    kk = kk / jnp.maximum(jnp.sqrt(_mm2l(kk * kk, seg_sum)), 1e-12)
    k_mod = k * (1.0 + (a - 1.0) * k_a)
    coef = _mm2l(r * k_mod * r_k, seg_sum)
    k_mod = jnp.where(valid, k_mod, 0.0)
    b = jnp.where(valid, kk * a, 0.0)

    incl = jnp.where(same_seq & (col <= row), 1.0, 0.0).astype(BF16)
    lw_hi = lw.astype(BF16)
    lw_mid = lw - lw_hi.astype(F32)
    lw_mid_b = lw_mid.astype(BF16)
    lw_lo = (lw_mid - lw_mid_b.astype(F32)).astype(BF16)
    cum = _dg(incl, lw_hi) + (_dg(incl, lw_mid_b) + _dg(incl, lw_lo))
    yield
    ends =[cum[(s + 1) * seq_len - 1:(s + 1) * seq_len, :] for s in range(n_seq)]
    cum_end = ends[0] if n_seq == 1 else jnp.concatenate(
        [jnp.broadcast_to(e, (seq_len, LANE)) for e in ends], axis=0)
    a_t = kk * jnp.exp(cum - lw)
    r_t = r * jnp.exp(cum)
    inv_p = jnp.exp(-cum)
    b_t = b * inv_p
    k_t = k_mod * inv_p
    to_end = jnp.exp(cum_end - cum)
    k_end = k_mod * to_end
    b_end = b * to_end
    cum_t = cum.T

    strict = same_seq & (col < row)
    incl_m = same_seq & (col <= row)
    lhs2 = jnp.concatenate([a_t, r_t], axis=0)
    n_mats, a_aks, a_rbs, a_rks = [], [], [], []
    rhs2 = jnp.concatenate([b_t, k_t], axis=0)
    for head_mask in (head_a, jnp.logical_not(head_a)):
        m = _mm_tok(jnp.where(head_mask, lhs2, 0.0), rhs2, NT)
        m_b, m_k = m[:, :CHUNK], m[:, CHUNK:]
        n_mats.append(jnp.where(strict, m_b[:CHUNK], 0.0))
        a_aks.append(jnp.where(strict, m_k[:CHUNK], 0.0))
        a_rbs.append(jnp.where(incl_m, m_b[CHUNK:], 0.0))
        a_rks.append(jnp.where(incl_m, m_k[CHUNK:], 0.0))
    yield
    t_invs = yield from _unit_lower_inverse_steps(n_mats, row, col, seq_log2)
    parts = list(zip(t_invs, a_aks, a_rbs, a_rks))

    def sel(xa, xb):
        return jnp.where(head_a, xa, xb)

    if n_seq == 1:
        st = _mm_state(lhs2, states[0])
        a_h, r_h = st[:CHUNK], st[CHUNK:]
    else:
        a_hs, r_hs = [], []
        for s in range(n_seq):
            sl = slice(s * seq_len, (s + 1) * seq_len)
            st = _mm_state(jnp.concatenate([a_t[sl], r_t[sl]], axis=0), states[s])
            a_hs.append(st[:seq_len])
            r_hs.append(st[seq_len:])
        a_h = jnp.concatenate(a_hs, axis=0)
        r_h = jnp.concatenate(r_hs, axis=0)

    rhs = a_h + sel(_mm_tok(parts[0][1], v), _mm_tok(parts[1][1], v))
    yield
    u = sel(_mm_tok(parts[0][0], rhs), _mm_tok(parts[1][0], rhs))
    yield
    vu = jnp.concatenate([v, u], axis=0)
    y = r_h + sel(
        _mm_tok(jnp.concatenate([parts[0][3], -parts[0][2]], axis=1), vu),
        _mm_tok(jnp.concatenate([parts[1][3], -parts[1][2]], axis=1), vu))

    new_states = []
    for s in range(n_seq):
        sl = slice(s * seq_len, (s + 1) * seq_len)
        upd = _mm_state(jnp.concatenate([k_end[sl], -b_end[sl]], axis=0),
                        jnp.concatenate([v[sl], u[sl]], axis=0), TN)
        p_end_col = jnp.exp(cum_t[:, (s + 1) * seq_len - 1:(s + 1) * seq_len])
        new_states.append(states[s] * p_end_col + jnp.where(same_head, upd, 0.0))
    yield

    mean = _mm2l(y, seg_avg)
    yc = y - mean
    yield
    var = _mm2l(yc * yc, seg_avg)
    yn = yc * lax.rsqrt(var + GN_EPS) * ln_w + ln_b
    return (yn + coef * v) * g, new_states


def _rwkv_kernel(r_ref, k_ref, v_ref, da_ref, g_ref, sr_ref, sk_ref, sv_ref, sda_ref, sg_ref,
                 hb0_ref, p_ref, mu2_ref, wd_ref, wa_ref, wg_ref, o_ref, hb_out_ref, *scratch,
                 n_pairs, seq_log2, first_valid):
    stacked = seq_log2 < 7
    rcol = lax.broadcasted_iota(jnp.int32, (CHUNK, 1), 0)
    if stacked:
        valid = rcol >= 0
        seq_start = (rcol & ((1 << seq_log2) - 1)) == 0
    else:
        hb_ref, carry_ref, carry2_ref = scratch
        c = pl.program_id(1)
        n_c = pl.num_programs(1)
        rglob = c * CHUNK + rcol
        valid = rglob >= first_valid

        @pl.when(c == 0)
        def _():
            hb_ref[...] = hb0_ref[0]
            for j, s_ref in enumerate((sr_ref, sk_ref, sv_ref)):
                carry_ref[j] = jnp.broadcast_to(s_ref[0, 0:1, :], carry_ref.shape[1:])
            for j, s_ref in enumerate((sda_ref, sg_ref)):
                carry2_ref[j] = jnp.broadcast_to(s_ref[0, 0:1, :], carry2_ref.shape[1:])

    P = p_ref[...]
    mu2 = mu2_ref[...]
    mus = (P[0:1], P[1:2], P[2:3], mu2[0:1], mu2[1:2])
    x_refs = (r_ref, k_ref, v_ref, da_ref, g_ref)
    s_refs = (sr_ref, sk_ref, sv_ref, sda_ref, sg_ref)
    us = []
    for j in range(5):
        x = x_refs[j][...]
        rolled = pltpu.roll(x, 1, axis=0)
        if stacked:
            prev = jnp.where(seq_start, s_refs[j][...], rolled)
        else:
            cref, jj = (carry_ref, j) if j < 3 else (carry2_ref, j - 3)
            prev = jnp.where(rcol == 0, cref[jj][SUB - 1:SUB, :], rolled)
            prev = jnp.where(rglob == first_valid, s_refs[j][0, 0:1, :], prev)
            cref[jj] = x[CHUNK - SUB:CHUNK, :]
        us.append(x + (prev - x) * mus[j])
    r, k, v, u_da, u_g = us

    xw = P[3:4] + _mm(jnp.tanh(u_da), wd_ref[...])
    w_log = -_softplus(-xw) - 0.5
    lw = jnp.where(valid, -jnp.exp(w_log), 0.0)
    a = jax.nn.sigmoid(P[4:5] + _mm(u_da, wa_ref[...]))
    g = _mm(jax.nn.sigmoid(u_g), wg_ref[...])

    gens = []
    for pr in range(n_pairs):
        sl = slice(pr * LANE, (pr + 1) * LANE)
        prm = tuple(P[i:i + 1, sl] for i in range(5, 10))
        if stacked:
            states = [hb0_ref[s, 0] for s in range(CHUNK >> seq_log2)]
        else:
            states = [hb_ref[pr]]
        gens.append(_rwkv_pair_steps(r[:, sl], k[:, sl], v[:, sl], lw[:, sl], a[:, sl], g[:, sl], prm,
                                     valid, states, seq_log2))
    for pr, (out, new_states) in enumerate(_run_interleaved(gens)):
        sl = slice(pr * LANE, (pr + 1) * LANE)
        o_ref[:, sl] = out
        if stacked:
            for s, st in enumerate(new_states):
                hb_out_ref[s, 0] = st
        else:
            hb_ref[pr] = new_states[0]

    if not stacked:
        @pl.when(c == n_c - 1)
        def _():
            hb_out_ref[0] = hb_ref[...]


def rwkv(proj, shift_rows, hb0, params, mu2, wd, wa, wg, seq_len, first_valid):
    n, _ = proj.shape
    n_pr = H_B // 2
    stacked = seq_len < CHUNK
    if stacked:
        assert CHUNK % seq_len == 0 and n % CHUNK == 0 and seq_len & (seq_len - 1) == 0
        seq_log2 = seq_len.bit_length() - 1
        n_pairs, per_step = 1, CHUNK // seq_len
        grid = (n // CHUNK, n_pr)
        rows = lambda g, pr: g
        pcol = lambda g, pr: pr
        hb_block = (per_step, 1, CHUNK, CHUNK)
        hb_map = lambda g, pr: (g, pr, 0, 0)
        s_rows = CHUNK
        srow = lambda g, pr: (g,)
        scratch = []
    else:
        assert seq_len % CHUNK == 0 and n % seq_len == 0
        seq_log2 = 7
        n_c = seq_len // CHUNK
        n_pairs = n_pr
        grid = (n // seq_len, n_c)
        rows = lambda bi, c: bi * n_c + c
        pcol = lambda bi, c: 0
        hb_block = (1, n_pr, CHUNK, CHUNK)
        hb_map = lambda bi, c: (bi, 0, 0, 0)
        s_rows = SUB
        srow = lambda bi, c: (bi, 0)
        scratch = [pltpu.VMEM((n_pr, CHUNK, CHUNK), F32), pltpu.VMEM((3, SUB, W_B), F32),
                   pltpu.VMEM((2, SUB, LANE), F32)]
    w = n_pairs * LANE
    q0 = 3 * W_A

    def x_spec(col0, width, per_pair):
        blk0 = col0 // width
        if per_pair:
            return pl.BlockSpec((CHUNK, width), lambda i, j: (rows(i, j), blk0 + pcol(i, j)))
        return pl.BlockSpec((CHUNK, width), lambda i, j: (rows(i, j), blk0))

    def s_spec(col0, width, per_pair):
        blk0 = col0 // width
        shape = (s_rows, width) if stacked else (1, s_rows, width)
        if per_pair:
            return pl.BlockSpec(shape, lambda i, j: srow(i, j) + (blk0 + pcol(i, j),))
        return pl.BlockSpec(shape, lambda i, j: srow(i, j) + (blk0,))

    cols = [(0, w, True), (W_B, w, True), (2 * W_B, w, True), (3 * W_B, LANE, False), (3 * W_B + LANE, LANE, False)]
    in_specs = ([x_spec(q0 + c0, wd_, pp) for c0, wd_, pp in cols]
                + [s_spec(c0, wd_, pp) for c0, wd_, pp in cols]
                + [pl.BlockSpec(hb_block, hb_map),
                   pl.BlockSpec((16, w), lambda i, j: (0, pcol(i, j))),
                   pl.BlockSpec((SUB, LANE), lambda i, j: (0, 0)),
                   pl.BlockSpec((LANE, w), lambda i, j: (0, pcol(i, j))),
                   pl.BlockSpec((LANE, w), lambda i, j: (0, pcol(i, j))),
                   pl.BlockSpec((LANE, w), lambda i, j: (0, pcol(i, j)))])
    kern = functools.partial(_rwkv_kernel, n_pairs=n_pairs, seq_log2=seq_log2, first_valid=first_valid)
    return pl.pallas_call(
        kern,
        grid=grid,
        in_specs=in_specs,
        out_specs=[pl.BlockSpec((CHUNK, w), lambda i, j: (rows(i, j), pcol(i, j))),
                   pl.BlockSpec(hb_block, hb_map)],
        out_shape=[jax.ShapeDtypeStruct((n, W_B), F32),
                   jax.ShapeDtypeStruct(hb0.shape, F32)],
        scratch_shapes=scratch,
        compiler_params=_cparams(("parallel", "arbitrary")),
        name="rwkv",
    )(proj, proj, proj, proj, proj, shift_rows, shift_rows, shift_rows, shift_rows, shift_rows,
      hb0, params, mu2, wd, wa, wg)


def _scan_rows(x, n):
    ridx = lax.broadcasted_iota(jnp.int32, (n, 1), 0)
    s = 1
    while s < n:
        x = x + jnp.where(ridx >= s, pltpu.roll(x, s, axis=0), 0.0)
        s *= 2
    return x


def _scan_lanes(x):
    lidx = lax.broadcasted_iota(jnp.int32, (1, LANE), 1)
    s = 1
    while s < LANE:
        x = x + jnp.where(lidx >= s, pltpu.roll(x, s, axis=1), 0.0)
        s *= 2
    return x


def _mlstm_kernel(q_ref, k_ref, v_ref, og_ref, gc_ref, gt_ref, conv0_ref, c0_ref, n0_ref, m0_ref,
                  cw_ref, cb_ref, gb_row_ref, gb_col_ref, nw_ref,
                  o_ref, c_out_ref, n_out_ref, m_out_ref,
                  cext_ref, m_ref, halo_ref, *, first_valid):
    c = pl.program_id(1)
    n_c = pl.num_programs(1)
    lq = q_ref.shape[1]
    rcol = lax.broadcasted_iota(jnp.int32, (lq, 1), 0)
    rglob = c * lq + rcol
    valid_c = rglob >= first_valid
    lrow = lax.broadcasted_iota(jnp.int32, (1, LANE), 1)
    valid_r = ((c * lq + lrow) >= first_valid) & (lrow < lq)
    causal = lax.broadcasted_iota(jnp.int32, (lq, LANE), 1) <= lax.broadcasted_iota(jnp.int32, (lq, LANE), 0)

    @pl.when(c == 0)
    def _():
        for h in range(H_C):
            cext_ref[h, :, :DV_C] = c0_ref[0, 0, h]
            cext_ref[h, :, DV_C:] = n0_ref[0, h]
        m_ref[...] = m0_ref[0]
        halo_ref[...] = conv0_ref[0]

    x = jnp.where(valid_c, jnp.concatenate([q_ref[0], k_ref[0]], axis=1), 0.0)
    xcat = jnp.concatenate([halo_ref[...], x], axis=0)
    cw = cw_ref[...]
    conv = cb_ref[...] + x * cw[CONV_W - 1:CONV_W]
    for dlt in range(1, CONV_W):
        conv = conv + pltpu.roll(xcat, dlt, axis=0)[SUB:SUB + lq] * cw[CONV_W - 1 - dlt:CONV_W - dlt]
    halo_ref[...] = xcat[lq:lq + SUB]
    qk = conv * jax.nn.sigmoid(conv)

    gcol = gc_ref[0] + gb_row_ref[...]
    lane_is_f = (lrow >= H_C) & (lrow < 2 * H_C)
    gcol = jnp.where(lane_is_f, -_softplus(-gcol), gcol)
    gcol = jnp.where(valid_c, gcol, jnp.where(lane_is_f, 0.0, -jnp.inf))
    bcum_cols = _scan_rows(jnp.where(lane_is_f, gcol, 0.0), lq)
    grow = gt_ref[0] + gb_col_ref[...]
    srow = lax.broadcasted_iota(jnp.int32, (SUB, 1), 0)
    row_is_f = srow >= H_C
    grow = jnp.where(row_is_f, -_softplus(-grow), grow)
    grow = jnp.where(valid_r, grow, jnp.where(row_is_f, 0.0, -jnp.inf))
    bcum_rows = _scan_lanes(jnp.where(row_is_f, grow, 0.0))

    ones_ext = jnp.ones((lq, LANE), F32)

    def head_steps(h):
        q = qk[:, h * DK_C:(h + 1) * DK_C]
        k = qk[:, QK_C + h * DK_C:QK_C + (h + 1) * DK_C] * (DK_C ** -0.5)
        v_ext = jnp.concatenate([v_ref[0, :, h * DV_C:(h + 1) * DV_C], ones_ext], axis=1)
        li_c = gcol[:, h:h + 1]
        bc_c = bcum_cols[:, H_C + h:H_C + h + 1]
        li_r = grow[h:h + 1, :]
        bc_r = bcum_rows[H_C + h:H_C + h + 1, :]
        m_prev = m_ref[h, 0:1, 0:1]
        cext = cext_ref[h]

        dmat = jnp.where(causal, bc_c - bc_r + li_r, -jnp.inf)
        inter = bc_c + m_prev
        m_row = jnp.maximum(inter, jnp.max(dmat, axis=1, keepdims=True))
        k_pad = _pad_rows(k, CHUNK)
        s = _mm(q, k_pad, NT)
        q_c = _mm(q, cext)
        yield
        w_intra = jnp.exp(dmat - m_row)
        w_inter = jnp.exp(inter - m_row)
        s = s * w_intra
        yield
        num_ext = _mm(s, _pad_rows(v_ext, CHUNK)) + w_inter * q_c
        den = num_ext[:, DV_C:DV_C + 1]
        hh = num_ext[:, :DV_C] / jnp.maximum(jnp.abs(den), jnp.exp(-m_row))

        m_new = m_row[lq - 1:lq, :]
        bc_last = bc_c[lq - 1:lq, :]
        w_end = jnp.exp(bc_last - bc_c + li_c - m_new)
        dec = jnp.exp(bc_last + m_prev - m_new)
        cext_ref[h] = dec * cext + _mm(k_pad, _pad_rows(w_end * v_ext, CHUNK), TN)
        m_ref[h] = jnp.broadcast_to(m_new, (SUB, LANE))
        yield

        mu = jnp.mean(hh, axis=1, keepdims=True)
        hc = hh - mu
        yield
        var = jnp.mean(hc * hc, axis=1, keepdims=True)
        hn = hc * lax.rsqrt(var + RMS_EPS) * nw_ref[:, h * DV_C:(h + 1) * DV_C]
        o_ref[0, :, h * DV_C:(h + 1) * DV_C] = hn * jax.nn.sigmoid(og_ref[0, :, h * DV_C:(h + 1) * DV_C])

    _run_interleaved([head_steps(h) for h in range(H_C)])

    @pl.when(c == n_c - 1)
    def _():
        for h in range(H_C):
            c_out_ref[0, h] = cext_ref[h, :, :DV_C]
            n_out_ref[0, h] = cext_ref[h, :, DV_C:]
        m_out_ref[0] = m_ref[...]


def mlstm(proj, gates_t, conv0, c0_all, layer, n0b, m0b, conv_w, conv_b, gb_row, gb_col, norm_w, first_valid):
    bsz, t, _ = proj.shape
    lq = min(t, CHUNK)
    n_c = t // lq
    assert t % lq == 0 and gates_t.shape[2] == n_c * CHUNK
    qk2 = 2 * QK_C
    kern = functools.partial(_mlstm_kernel, first_valid=first_valid)
    const2 = lambda bi, c: (0, 0)
    return pl.pallas_call(
        kern,
        grid=(bsz, n_c),
        in_specs=[pl.BlockSpec((1, lq, QK_C), lambda bi, c: (bi, c, 0)),
                  pl.BlockSpec((1, lq, QK_C), lambda bi, c: (bi, c, 1)),
                  pl.BlockSpec((1, lq, V_C), lambda bi, c: (bi, c, qk2 // V_C)),
                  pl.BlockSpec((1, lq, V_C), lambda bi, c: (bi, c, qk2 // V_C + 1)),
                  pl.BlockSpec((1, lq, LANE), lambda bi, c: (bi, c, (qk2 + 2 * V_C) // LANE)),
                  pl.BlockSpec((1, SUB, LANE), lambda bi, c: (bi, 0, c)),
                  pl.BlockSpec((1, SUB, qk2), lambda bi, c: (bi, 0, 0)),
                  pl.BlockSpec((1, 1, H_C, DK_C, DV_C), lambda bi, c: (layer, bi, 0, 0, 0)),
                  pl.BlockSpec((1, H_C, DK_C, LANE), lambda bi, c: (bi, 0, 0, 0)),
                  pl.BlockSpec((1, H_C, SUB, LANE), lambda bi, c: (bi, 0, 0, 0)),
                  pl.BlockSpec((SUB, qk2), const2),
                  pl.BlockSpec((1, qk2), const2),
                  pl.BlockSpec((1, LANE), const2),
                  pl.BlockSpec((SUB, LANE), const2),
                  pl.BlockSpec((1, V_C), const2)],
        out_specs=[pl.BlockSpec((1, lq, V_C), lambda bi, c: (bi, c, 0)),
                   pl.BlockSpec((1, H_C, DK_C, DV_C), lambda bi, c: (bi, 0, 0, 0)),
                   pl.BlockSpec((1, H_C, DK_C, LANE), lambda bi, c: (bi, 0, 0, 0)),
                   pl.BlockSpec((1, H_C, SUB, LANE), lambda bi, c: (bi, 0, 0, 0))],
        out_shape=[jax.ShapeDtypeStruct((bsz, t, V_C), F32),
                   jax.ShapeDtypeStruct((bsz, H_C, DK_C, DV_C), F32),
                   jax.ShapeDtypeStruct((bsz, H_C, DK_C, LANE), F32),
                   jax.ShapeDtypeStruct((bsz, H_C, SUB, LANE), F32)],
        scratch_shapes=[pltpu.VMEM((H_C, DK_C, DV_C + LANE), F32),
                        pltpu.VMEM((H_C, SUB, LANE), F32),
                        pltpu.VMEM((SUB, qk2), F32)],
        compiler_params=_cparams(("parallel", "arbitrary")),
        name="mlstm",
    )(proj, proj, proj, proj, proj, gates_t, conv0, c0_all, n0b, m0b, conv_w, conv_b, gb_row, gb_col, norm_w)


def _pad_to(x, axis, size):
    pad = [(0, 0)] * x.ndim
    pad[axis] = (0, size - x.shape[axis])
    return jnp.pad(x, pad)


def _rwkv_params(i, rwkv_mu, rwkv_w0, rwkv_a0, rwkv_k_k, rwkv_k_a, rwkv_r_k, rwkv_ln_w, rwkv_ln_b,
                 w_decay_up, w_iclr_up, w_gate_up):
    mu = rwkv_mu[i]
    rows = [mu[0:W_B], mu[W_B:2 * W_B], mu[2 * W_B:3 * W_B], rwkv_w0[i], rwkv_a0[i], rwkv_k_k[i],
            rwkv_k_a[i], rwkv_r_k[i], rwkv_ln_w[i], rwkv_ln_b[i]]
    pp = _pad_to(jnp.stack(rows, axis=0), 0, 16)
    mu2 = _pad_to(mu[3 * W_B:].reshape(2, LANE), 0, SUB)
    wd = jnp.concatenate([w_decay_up[i], jnp.zeros((R_ICLR, W_B), F32)], axis=0).astype(BF16)
    wa = jnp.concatenate([jnp.zeros((R_DECAY, W_B), F32), w_iclr_up[i]], axis=0).astype(BF16)
    wg = w_gate_up[i].astype(BF16)
    return pp, mu2, wd, wa, wg


def _pair_states(wkv):
    bsz = wkv.shape[0]
    st = jnp.swapaxes(wkv, -1, -2).reshape(bsz, H_B // 2, 2, DH_B, DH_B)
    z = jnp.zeros_like(st[:, :, 0])
    top = jnp.concatenate([st[:, :, 0], z], axis=-1)
    bot = jnp.concatenate([z, st[:, :, 1]], axis=-1)
    return jnp.concatenate([top, bot], axis=-2)


def _unpair_states(hb):
    bsz = hb.shape[0]
    a = hb[:, :, :DH_B, :DH_B]
    b = hb[:, :, DH_B:, DH_B:]
    st = jnp.stack([a, b], axis=2).reshape(bsz, H_B, DH_B, DH_B)
    return jnp.swapaxes(st, -1, -2)


def kernel(x_prompt, x_sample, cache_k, cache_v, page_table, state_wkv, state_shift, state_mlstm_c, state_mlstm_n, state_mlstm_m, state_conv, meta_tokens, g_mix_pre, g_mix_post, g_ffn_pre, g_ffn_post, w_in_ab, w_out_ab, sb_bias, rwkv_mu, rwkv_w0, rwkv_w_decay_up, rwkv_a0, rwkv_w_iclr_up, rwkv_w_gate_up, rwkv_k_k, rwkv_k_a, rwkv_r_k, rwkv_ln_w, rwkv_ln_b, w_in_c, conv_w, conv_b, mlstm_b_i, mlstm_b_f, mlstm_norm_w, w_out_c, w_ffn_gate, w_ffn_up, w_ffn_down):
    bsz, seq, d = x_prompt.shape
    db, ds, _ = x_sample.shape
    depth = g_mix_pre.shape[0]
    tp = PAD_FRONT + N_META + seq
    assert seq % CHUNK == 0 and ds % SUB == 0 and ds <= CHUNK and seq >= CONV_W and ds >= CONV_W

    meta = jnp.broadcast_to(meta_tokens.astype(F32)[None], (bsz, N_META, d))
    xp = jnp.concatenate([jnp.zeros((bsz, PAD_FRONT, d), F32), meta, x_prompt], axis=1).reshape(bsz * tp, d)
    xs = x_sample.reshape(db * ds, d)

    n_layers_ab, n_pool = cache_k.shape[:2]
    pool_k = jnp.transpose(cache_k, (0, 1, 3, 4, 2)).reshape(n_layers_ab, n_pool, W_A, CHUNK)
    pool_v = jnp.transpose(cache_v, (0, 1, 3, 4, 2)).reshape(n_layers_ab, n_pool, W_A, CHUNK)
    outs = {name: [] for name in ("k_p", "v_p", "wkv_p", "sh_p", "c_p", "n_p", "m_p", "cv_p",
                                  "k_s", "v_s", "wkv_s", "sh_s", "c_s", "n_s", "m_s", "cv_s")}
    row = lambda v: v.reshape(1, -1)

    for layer in range(depth):
        i = layer // 2
        if layer % 2 == 0:
            w_in = w_in_ab[i].astype(BF16)
            w_out = w_out_ab[i].astype(BF16)
            pp, mu2, wd, wa, wg = _rwkv_params(i, rwkv_mu, rwkv_w0, rwkv_a0, rwkv_k_k, rwkv_k_a, rwkv_r_k,
                                               rwkv_ln_w, rwkv_ln_b, rwkv_w_decay_up, rwkv_w_iclr_up, rwkv_w_gate_up)
            new_x = []
            for grp, x in (("p", xp), ("s", xs)):
                nb, t = (bsz, tp) if grp == "p" else (db, ds)
                proj = norm_matmul(x, row(g_mix_pre[layer]), w_in).reshape(nb, t, AB_COLS)
                if grp == "p":
                    o_a = sb_prompt(proj, sb_bias[i], PAD_FRONT)
                    shift_rows = jnp.zeros((nb, SUB, B_COLS), F32)
                    hb0 = jnp.zeros((nb, H_B // 2, CHUNK, CHUNK), F32)
                    first_valid = PAD_FRONT
                else:
                    o_a = sb_sample(proj, pool_k, pool_v, i, page_table, sb_bias[i])
                    shift_rows = _pad_to(state_shift[i][:, None, :], 1, t).reshape(nb * t, B_COLS)
                    hb0 = _pair_states(state_wkv[i])
                    first_valid = 0
                o_b, hb = rwkv(proj.reshape(nb * t, AB_COLS), shift_rows, hb0, pp, mu2, wd, wa, wg, t, first_valid)
                new_x.append(proj_res([o_a.reshape(nb * t, W_A), o_b.reshape(nb * t, W_B)],
                                      [w_out[:W_A], w_out[W_A:]], row(g_mix_post[layer]), x))
                outs["k_" + grp].append(proj[:, first_valid:, W_A:2 * W_A].reshape(nb, t - first_valid, H_A, DH_A))
                outs["v_" + grp].append(proj[:, first_valid:, 2 * W_A:3 * W_A].reshape(nb, t - first_valid, H_A, DH_A))
                outs["wkv_" + grp].append(_unpair_states(hb))
                outs["sh_" + grp].append(proj[:, t - 1, 3 * W_A:])
            xp, xs = new_x
        else:
            wc = w_in_c[i]
            ncol = wc.shape[1]
            wperm = jnp.concatenate([wc[:, :2 * QK_C + V_C], wc[:, ncol - V_C:],
                                     wc[:, 2 * QK_C + V_C:ncol - V_C]], axis=1)
            wperm = _pad_to(wperm, 1, C_COLS_PAD).astype(BF16)
            w_out = w_out_c[i].astype(BF16)
            cw = _pad_to(conv_w[i], 0, SUB)
            gb = jnp.concatenate([mlstm_b_i[i], mlstm_b_f[i]])
            gb_row = _pad_to(gb, 0, LANE).reshape(1, LANE)
            gb_col = jnp.broadcast_to(gb[:, None], (SUB, LANE))
            new_x = []
            for grp, x in (("p", xp), ("s", xs)):
                nb, t = (bsz, tp) if grp == "p" else (db, ds)
                proj = norm_matmul(x, row(g_mix_pre[layer]), wperm).reshape(nb, t, C_COLS_PAD)
                g0 = 2 * QK_C + 2 * V_C
                gates_t = jnp.swapaxes(proj[:, :, g0:g0 + SUB], 1, 2)
                if grp == "p":
                    first_valid = PAD_FRONT
                    conv0 = jnp.zeros((nb, SUB, 2 * QK_C), F32)
                    c0_all, c0_layer = jnp.zeros((1, nb, H_C, DK_C, DV_C), F32), 0
                    n0 = jnp.zeros((nb, H_C, DK_C), F32)
                    m0 = jnp.zeros((nb, H_C), F32)
                else:
                    first_valid = 0
                    gates_t = _pad_to(gates_t, 2, CHUNK)
                    conv0 = jnp.concatenate([jnp.zeros((nb, SUB - (CONV_W - 1), 2 * QK_C), F32), state_conv[i]], axis=1)
                    c0_all, c0_layer = state_mlstm_c, i
                    n0, m0 = state_mlstm_n[i], state_mlstm_m[i]
                n0b = jnp.broadcast_to(n0[..., None], (nb, H_C, DK_C, LANE))
                m0b = jnp.broadcast_to(m0[..., None, None], (nb, H_C, SUB, LANE))
                hmix, c_new, n_new, m_new = mlstm(proj, gates_t, conv0, c0_all, c0_layer, n0b, m0b, cw,
                                                  row(conv_b[i]), gb_row, gb_col, row(mlstm_norm_w[i]), first_valid)
                new_x.append(proj_res([hmix.reshape(nb * t, V_C)], [w_out], row(g_mix_post[layer]), x))
                outs["c_" + grp].append(c_new)
                outs["n_" + grp].append(n_new[..., 0])
                outs["m_" + grp].append(m_new[..., 0, 0])
                outs["cv_" + grp].append(proj[:, t - (CONV_W - 1):, :2 * QK_C])
            xp, xs = new_x
        wgt, wup, wdn = (w_ffn_gate[layer].astype(BF16), w_ffn_up[layer].astype(BF16),
                         w_ffn_down[layer].astype(BF16))
        xp = ffn(xp, row(g_ffn_pre[layer]), wgt, wup, wdn, row(g_ffn_post[layer]))
        xs = ffn(xs, row(g_ffn_pre[layer]), wgt, wup, wdn, row(g_ffn_post[layer]))

    y_prompt = xp.reshape(bsz, tp, d)[:, PAD_FRONT + N_META:]
    y_sample = xs.reshape(db, ds, d)
    st = lambda name: jnp.stack(outs[name])
    return (y_prompt, y_sample, st("k_p"), st("v_p"), st("wkv_p"), st("sh_p"), st("c_p"), st("n_p"),
            st("m_p"), st("cv_p"), st("k_s"), st("v_s"), st("wkv_s"), st("sh_s"), st("c_s"), st("n_s"),
            st("m_s"), st("cv_s"))
```

```python
import functools

import jax
import jax.numpy as jnp
from jax import lax
from jax.experimental import pallas as pl
from jax.experimental.pallas import tpu as pltpu

F32 = jnp.float32
BF16 = jnp.bfloat16

D_MODEL = 1024
N_META = 16
H_A, DH_A = 8, 64
W_A = H_A * DH_A
H_B, DH_B = 8, 64
W_B = H_B * DH_B
R_DECAY, R_ICLR, R_GATE = 64, 64, 128
B_COLS = 3 * W_B + R_DECAY + R_ICLR + R_GATE
AB_COLS = 3 * W_A + B_COLS
H_C, DK_C, DV_C = 4, 128, 256
QK_C = H_C * DK_C
V_C = H_C * DV_C
CONV_W = 4
C_COLS_PAD = 2 * QK_C + 2 * V_C + 128
RMS_EPS = 1e-6
GN_EPS = 64e-5
LOG2E = 1.4426950408889634

LANE = 128
SUB = 8
CHUNK = 128
PAD_FRONT = CHUNK - N_META
VMEM_LIMIT = 56 * 1024 * 1024

NN = (((1,), (0,)), ((), ()))
NT = (((1,), (1,)), ((), ()))
TN = (((0,), (0,)), ((), ()))


def _cparams(sem):
    return pltpu.CompilerParams(dimension_semantics=sem, vmem_limit_bytes=VMEM_LIMIT)


def _dg(a, b, dims=NN):
    return lax.dot_general(a, b, dims, preferred_element_type=F32)


def _split(x):
    hi = x.astype(BF16)
    lo = (x - hi.astype(F32)).astype(BF16)
    return hi, lo


def _mm(a, b, dims=NN):
    return _dg(a.astype(BF16), b.astype(BF16), dims)


def _mm3(a, b, dims=NN):
    ah, al = _split(a)
    bh, bl = _split(b)
    return _dg(ah, bh, dims) + (_dg(ah, bl, dims) + _dg(al, bh, dims))


def _mm2l(a, b_bf16, dims=NN):
    ah, al = _split(a)
    return _dg(ah, b_bf16, dims) + _dg(al, b_bf16, dims)


_mm_tok = _mm
_mm_state = _mm


def _rms(x, g):
    ms = jnp.mean(x * x, axis=-1, keepdims=True)
    return x * lax.rsqrt(ms + RMS_EPS) * g


def _softplus(z):
    return jnp.maximum(z, 0.0) + jnp.log1p(jnp.exp(-jnp.abs(z)))


def _pick_tm(n, cap=512):
    for tm in (512, 384, 256, 128, 64, 32, 16, 8):
        if tm <= cap and n % tm == 0:
            return tm
    raise ValueError(f"row count {n} not a multiple of 8")


def _pad_rows(x, rows):
    if x.shape[0] == rows:
        return x
    return jnp.concatenate([x, jnp.zeros((rows - x.shape[0],) + x.shape[1:], x.dtype)], axis=0)


def _norm_matmul_kernel(x_ref, g_ref, w_ref, o_ref):
    h = _rms(x_ref[...], g_ref[...]).astype(BF16)
    o_ref[...] = _dg(h, w_ref[...])


def norm_matmul(x, g, w):
    n, d = x.shape
    c = w.shape[1]
    tm = _pick_tm(n)
    return pl.pallas_call(
        _norm_matmul_kernel,
        grid=(n // tm,),
        in_specs=[pl.BlockSpec((tm, d), lambda i: (i, 0)),
                  pl.BlockSpec((1, d), lambda i: (0, 0)),
                  pl.BlockSpec((d, c), lambda i: (0, 0))],
        out_specs=pl.BlockSpec((tm, c), lambda i: (i, 0)),
        out_shape=jax.ShapeDtypeStruct((n, c), F32),
        compiler_params=_cparams(("parallel",)),
        name="norm_matmul",
    )(x, g, w)


def _proj_res_kernel(*refs, n_in):
    ins, ws = refs[:n_in], refs[n_in:2 * n_in]
    g_ref, x_ref, o_ref = refs[2 * n_in:]
    acc = _dg(ins[0][...].astype(BF16), ws[0][...])
    for a, w in zip(ins[1:], ws[1:]):
        acc = acc + _dg(a[...].astype(BF16), w[...])
    o_ref[...] = x_ref[...] + _rms(acc, g_ref[...])


def proj_res(ins, ws, g, x):
    n, d = x.shape
    tm = _pick_tm(n)
    n_in = len(ins)
    in_specs = ([pl.BlockSpec((tm, a.shape[1]), lambda i: (i, 0)) for a in ins]
                + [pl.BlockSpec(w.shape, lambda i: (0, 0)) for w in ws]
                + [pl.BlockSpec((1, d), lambda i: (0, 0)), pl.BlockSpec((tm, d), lambda i: (i, 0))])
    return pl.pallas_call(
        functools.partial(_proj_res_kernel, n_in=n_in),
        grid=(n // tm,),
        in_specs=in_specs,
        out_specs=pl.BlockSpec((tm, d), lambda i: (i, 0)),
        out_shape=jax.ShapeDtypeStruct((n, d), F32),
        compiler_params=_cparams(("parallel",)),
        name="proj_res",
    )(*ins, *ws, g, x)


def _ffn_kernel(x_ref, gpre_ref, wg_ref, wu_ref, wd_ref, gpost_ref, o_ref, *, n_split):
    x = x_ref[...]
    h = _rms(x, gpre_ref[...]).astype(BF16)
    d_ff = wg_ref.shape[1]
    step = d_ff // n_split
    ff = None
    for s in range(n_split):
        gate = _dg(h, wg_ref[:, s * step:(s + 1) * step])
        up = _dg(h, wu_ref[:, s * step:(s + 1) * step])
        a = (gate * jax.nn.sigmoid(gate) * up).astype(BF16)
        part = _dg(a, wd_ref[s * step:(s + 1) * step, :])
        ff = part if ff is None else ff + part
    o_ref[...] = x + _rms(ff, gpost_ref[...])


def ffn(x, gpre, wg, wu, wd, gpost):
    n, d = x.shape
    d_ff = wg.shape[1]
    tm = _pick_tm(n, 256)
    n_split = 2 if d_ff % (2 * LANE) == 0 else 1
    const = lambda i: (0, 0)
    return pl.pallas_call(
        functools.partial(_ffn_kernel, n_split=n_split),
        grid=(n // tm,),
        in_specs=[pl.BlockSpec((tm, d), lambda i: (i, 0)),
                  pl.BlockSpec((1, d), const),
                  pl.BlockSpec((d, d_ff), const),
                  pl.BlockSpec((d, d_ff), const),
                  pl.BlockSpec((d_ff, d), const),
                  pl.BlockSpec((1, d), const)],
        out_specs=pl.BlockSpec((tm, d), lambda i: (i, 0)),
        out_shape=jax.ShapeDtypeStruct((n, d), F32),
        compiler_params=_cparams(("parallel",)),
        name="ffn",
    )(x, gpre, wg, wu, wd, gpost)


def _later_strict(n_heads):
    n = n_heads * CHUNK
    j = lax.broadcasted_iota(jnp.int32, (n, n), 0)
    s = lax.broadcasted_iota(jnp.int32, (n, n), 1)
    return jnp.where(((j >> 7) == (s >> 7)) & (j > s), 1.0, 0.0).astype(BF16)


def _sb_terms2(z2, mask):
    neg_abs = lax.bitcast_convert_type(lax.bitcast_convert_type(z2, jnp.uint32) | jnp.uint32(0x80000000), F32)
    sp = jnp.maximum(z2, 0.0) + jnp.log2(1.0 + jnp.exp2(neg_abs))
    lb = z2 - sp
    if mask is not None:
        sp = jnp.where(mask, sp, 0.0)
    return lb, sp


def _sb_prompt_kernel(bias_ref, q_ref, k_ref, v_ref, o_ref, acc_ref, ra_ref, rb_ref, *, first_valid, scale):
    hp = pl.program_id(1)
    qi = pl.program_id(2)
    tq = q_ref.shape[1]
    q = (q_ref[0] * (scale * LOG2E)).astype(BF16)
    head_a = lax.broadcasted_iota(jnp.int32, (1, LANE), 1) < DH_A
    head_a2 = lax.broadcasted_iota(jnp.int32, (1, 2 * CHUNK), 1) < CHUNK
    bias2 = jnp.where(head_a2, bias_ref[2 * hp], bias_ref[2 * hp + 1]) * LOG2E
    later = _later_strict(2)
    acc_ref[...] = jnp.zeros_like(acc_ref)
    ra_ref[...] = jnp.zeros_like(ra_ref)
    rb_ref[...] = jnp.zeros_like(rb_ref)

    n_grp = tq // CHUNK

    def process_group(j0, masked):
        kjs = [j0 + g for g in reversed(range(n_grp))]
        starts = [pl.multiple_of(kj * CHUNK, CHUNK) for kj in kjs]
        masks = [None] * n_grp
        if masked:
            t_pos = qi * tq + lax.broadcasted_iota(jnp.int32, (tq, 2 * CHUNK), 0)
            s_loc = lax.broadcasted_iota(jnp.int32, (tq, 2 * CHUNK), 1) & (CHUNK - 1)
            masks = [kj * CHUNK + s_loc < t_pos for kj in kjs]
        z2s = []
        for start in starts:
            k = k_ref[0, pl.ds(start, CHUNK), :]
            k2 = jnp.concatenate([jnp.where(head_a, k, 0.0), jnp.where(head_a, 0.0, k)], axis=0).astype(BF16)
            z2s.append(_dg(q, k2, NT))
        tiles = []
        for z2, mask in zip(z2s, masks):
            lb, sp = _sb_terms2(z2 + bias2, mask)
            within = _dg(sp.astype(BF16), later)
            rs_a = jnp.sum(sp[:, :CHUNK], axis=1, keepdims=True)
            rs_b = jnp.sum(sp[:, CHUNK:], axis=1, keepdims=True)
            tiles.append((lb - within, rs_a, rs_b))
        ra, rb, acc = ra_ref[...], rb_ref[...], acc_ref[...]
        for (ex, rs_a, rs_b), mask, kj, start in zip(tiles, masks, kjs, starts):
            att = jnp.exp2(jnp.concatenate([ex[:, :CHUNK] - ra, ex[:, CHUNK:] - rb], axis=1))
            if mask is not None:
                att = jnp.where(mask, att, 0.0)
            krow = kj * CHUNK + lax.broadcasted_iota(jnp.int32, (CHUNK, 1), 0)
            v = jnp.where(krow >= first_valid, v_ref[0, pl.ds(start, CHUNK), :], 0.0)
            v2 = jnp.concatenate([jnp.where(head_a, v, 0.0), jnp.where(head_a, 0.0, v)], axis=0).astype(BF16)
            acc = acc + _dg(att.astype(BF16), v2)
            ra = ra + rs_a
            rb = rb + rs_b
        ra_ref[...], rb_ref[...], acc_ref[...] = ra, rb, acc

    process_group(qi * n_grp, True)

    def body(i, carry):
        process_group((qi - 1 - i) * n_grp, False)
        return carry

    lax.fori_loop(0, qi, body, 0)
    o_ref[0] = acc_ref[...]


def sb_prompt(proj, bias, first_valid):
    b, t, _ = proj.shape
    tq = 3 * CHUNK if t % (3 * CHUNK) == 0 else CHUNK
    n_hp = H_A // 2
    kern = functools.partial(_sb_prompt_kernel, first_valid=first_valid, scale=DH_A ** -0.5)
    return pl.pallas_call(
        kern,
        grid=(b, n_hp, t // tq),
        in_specs=[pl.BlockSpec(memory_space=pltpu.SMEM),
                  pl.BlockSpec((1, tq, LANE), lambda bi, h, qi: (bi, qi, h)),
                  pl.BlockSpec((1, t, LANE), lambda bi, h, qi: (bi, 0, n_hp + h)),
                  pl.BlockSpec((1, t, LANE), lambda bi, h, qi: (bi, 0, 2 * n_hp + h))],
        out_specs=pl.BlockSpec((1, tq, LANE), lambda bi, h, qi: (bi, qi, h)),
        out_shape=jax.ShapeDtypeStruct((b, t, W_A), F32),
        scratch_shapes=[pltpu.VMEM((tq, LANE), F32)] * 3,
        compiler_params=_cparams(("parallel", "parallel", "arbitrary")),
        name="sb_prompt",
    )(bias, proj, proj, proj)


def _sb_sample_kernel(pt_ref, q_ref, kn_ref, vn_ref, bias_ref, *rest, scale, n_g):
    kp_refs, vp_refs = rest[:n_g], rest[n_g:2 * n_g]
    o_ref, qbd_ref, acc_ref, r_ref = rest[2 * n_g:]
    p = pl.program_id(1)
    n_steps = pl.num_programs(1)
    tq = q_ref.shape[1]
    rows = H_A * tq
    later = _later_strict(1)
    col_head = lax.broadcasted_iota(jnp.int32, (1, W_A), 1) >> 6
    bias2 = bias_ref[...] * LOG2E

    def scores(z2s, mask):
        terms = [_sb_terms2(z2 + bias2, mask) for z2 in z2s]
        withins = [_dg(sp.astype(BF16), later) for _, sp in terms]
        return [(lb - within, jnp.sum(sp, axis=1, keepdims=True)) for (lb, sp), within in zip(terms, withins)]

    @pl.when(p == 0)
    def _():
        q = q_ref[0] * (scale * LOG2E)
        qbd_ref[...] = jnp.concatenate(
            [jnp.where(col_head == h, q, 0.0) for h in range(H_A)], axis=0).astype(BF16)
        kn = _pad_rows(kn_ref[0], CHUNK).astype(BF16)
        vn = _pad_rows(vn_ref[0], CHUNK).astype(BF16)
        t_pos = lax.broadcasted_iota(jnp.int32, (rows, CHUNK), 0) & (tq - 1)
        s_pos = lax.broadcasted_iota(jnp.int32, (rows, CHUNK), 1)
        mask = s_pos < t_pos
        (ex, rs), = scores([_dg(qbd_ref[...], kn, NT)], mask)
        att = jnp.where(mask, jnp.exp2(ex), 0.0)
        acc_ref[...] = _dg(att.astype(BF16), vn)
        r_ref[...] = jnp.broadcast_to(rs, r_ref.shape)

    qbd = qbd_ref[...]
    parts = scores([_dg(qbd, kp_refs[j][0, 0].astype(BF16)) for j in range(n_g)], None)
    r = r_ref[...]
    atts = []
    for ex, rs in parts:
        atts.append(jnp.exp2(ex - r).astype(BF16))
        r = r + rs
    acc = acc_ref[...]
    for j in range(n_g):
        acc = acc + _dg(atts[j], vp_refs[j][0, 0].astype(BF16), NT)
    acc_ref[...] = acc
    r_ref[...] = r

    @pl.when(p == n_steps - 1)
    def _():
        out = jnp.where(col_head == 0, acc[0:tq], 0.0)
        for h in range(1, H_A):
            out = out + jnp.where(col_head == h, acc[h * tq:(h + 1) * tq], 0.0)
        o_ref[0] = out


def sb_sample(proj, pool_k, pool_v, layer, page_table, bias):
    db, tq, _ = proj.shape
    n_pages = page_table.shape[1]
    page = pool_k.shape[3]
    assert page == CHUNK and tq % SUB == 0 and tq <= CHUNK and tq & (tq - 1) == 0 and DH_A == 64 and DH_B == 64
    n_g = next(g for g in (16, 8, 4, 2, 1) if n_pages % g == 0)
    rows = H_A * tq
    bias_rows = jnp.broadcast_to(jnp.repeat(bias, tq)[:, None], (rows, CHUNK)).astype(F32)
    pt = page_table.reshape(-1).astype(jnp.int32)

    def page_spec(j):
        return pl.BlockSpec((1, 1, W_A, page), lambda bi, p, pt_ref: (
            layer, pt_ref[bi * n_pages + (n_pages - 1 - (p * n_g + j))], 0, 0))

    grid_spec = pltpu.PrefetchScalarGridSpec(
        num_scalar_prefetch=1,
        grid=(db, n_pages // n_g),
        in_specs=([pl.BlockSpec((1, tq, W_A), lambda bi, p, pt_ref: (bi, 0, 0)),
                   pl.BlockSpec((1, tq, W_A), lambda bi, p, pt_ref: (bi, 0, 1)),
                   pl.BlockSpec((1, tq, W_A), lambda bi, p, pt_ref: (bi, 0, 2)),
                   pl.BlockSpec((rows, CHUNK), lambda bi, p, pt_ref: (0, 0))]
                  + [page_spec(j) for j in range(n_g)] * 2),
        out_specs=pl.BlockSpec((1, tq, W_A), lambda bi, p, pt_ref: (bi, 0, 0)),
        scratch_shapes=[pltpu.VMEM((rows, W_A), BF16),
                        pltpu.VMEM((rows, W_A), F32),
                        pltpu.VMEM((rows, CHUNK), F32)],
    )
    return pl.pallas_call(
        functools.partial(_sb_sample_kernel, scale=DH_A ** -0.5, n_g=n_g),
        grid_spec=grid_spec,
        out_shape=jax.ShapeDtypeStruct((db, tq, W_A), F32),
        compiler_params=_cparams(("parallel", "arbitrary")),
        name="sb_sample",
    )(pt, proj, proj, proj, bias_rows, *([pool_k] * n_g), *([pool_v] * n_g))


def _run_interleaved(gens):
    results = [None] * len(gens)
    active = list(range(len(gens)))
    while active:
        for i in list(active):
            try:
                next(gens[i])
            except StopIteration as stop:
                results[i] = stop.value
                active.remove(i)
    return results


def _unit_lower_inverse_steps(n_mats, row, col, seq_log2):
    eye = jnp.where(row == col, 1.0, 0.0)
    log_base = min(4, seq_log2)
    ds = [jnp.where((row >> log_base) == (col >> log_base), n, 0.0) for n in n_mats]
    xs = [eye - d for d in ds]
    for _ in range(log_base - 1):
        ds = [_mm(d, d) for d in ds]
        yield
        xs = [x + _mm(x, d) for x, d in zip(xs, ds)]
        yield
    for lg in range(log_base, seq_log2):
        off = (row >> (lg + 1)) == (col >> (lg + 1))
        off = jnp.where(off, (row >> lg) & 1, 0) > ((col >> lg) & 1)
        ts = [_mm(jnp.where(off, n, 0.0), x) for n, x in zip(n_mats, xs)]
        yield
        xs = [x - _mm(x, t) for x, t in zip(xs, ts)]
        yield
    return xs


def _rwkv_pair_steps(r, k, v, lw, a, g, prm, valid, states, seq_log2):
    k_k, k_a, r_k, ln_w, ln_b = prm
    seq_len = 1 << seq_log2
    n_seq = CHUNK >> seq_log2
    row = lax.broadcasted_iota(jnp.int32, (CHUNK, CHUNK), 0)
    col = lax.broadcasted_iota(jnp.int32, (CHUNK, CHUNK), 1)
    head_a = lax.broadcasted_iota(jnp.int32, (1, LANE), 1) < DH_B
    same_head = (row >> 6) == (col >> 6)
    same_seq = (row >> seq_log2) == (col >> seq_log2)
    seg_sum = jnp.where(same_head, 1.0, 0.0).astype(BF16)
    seg_avg = jnp.where(same_head, 1.0 / DH_B, 0.0).astype(BF16)

    kk = k * k_k
    kk = kk / jnp.maximum(jnp.sqrt(_mm2l(kk * kk, seg_sum)), 1e-12)
    k_mod = k * (1.0 + (a - 1.0) * k_a)
    coef = _mm2l(r * k_mod * r_k, seg_sum)
    k_mod = jnp.where(valid, k_mod, 0.0)
    b = jnp.where(valid, kk * a, 0.0)

    incl = jnp.where(same_seq & (col <= row), 1.0, 0.0).astype(BF16)
    lw_hi = lw.astype(BF16)
    lw_mid = lw - lw_hi.astype(F32)
    lw_mid_b = lw_mid.astype(BF16)
    lw_lo = (lw_mid - lw_mid_b.astype(F32)).astype(BF16)
    cum = _dg(incl, lw_hi) + (_dg(incl, lw_mid_b) + _dg(incl, lw_lo))
    yield
    ends =[cum[(s + 1) * seq_len - 1:(s + 1) * seq_len, :] for s in range(n_seq)]
    cum_end = ends[0] if n_seq == 1 else jnp.concatenate(
        [jnp.broadcast_to(e, (seq_len, LANE)) for e in ends], axis=0)
    a_t = kk * jnp.exp(cum - lw)
    r_t = r * jnp.exp(cum)
    inv_p = jnp.exp(-cum)
    b_t = b * inv_p
    k_t = k_mod * inv_p
    to_end = jnp.exp(cum_end - cum)
    k_end = k_mod * to_end
    b_end = b * to_end
    cum_t = cum.T

    strict = same_seq & (col < row)
    incl_m = same_seq & (col <= row)
    lhs2 = jnp.concatenate([a_t, r_t], axis=0)
    n_mats, a_aks, a_rbs, a_rks = [], [], [], []
    rhs2 = jnp.concatenate([b_t, k_t], axis=0)
    for head_mask in (head_a, jnp.logical_not(head_a)):
        m = _mm_tok(jnp.where(head_mask, lhs2, 0.0), rhs2, NT)
        m_b, m_k = m[:, :CHUNK], m[:, CHUNK:]
        n_mats.append(jnp.where(strict, m_b[:CHUNK], 0.0))
        a_aks.append(jnp.where(strict, m_k[:CHUNK], 0.0))
        a_rbs.append(jnp.where(incl_m, m_b[CHUNK:], 0.0))
        a_rks.append(jnp.where(incl_m, m_k[CHUNK:], 0.0))
    yield
    t_invs = yield from _unit_lower_inverse_steps(n_mats, row, col, seq_log2)
    parts = list(zip(t_invs, a_aks, a_rbs, a_rks))

    def sel(xa, xb):
        return jnp.where(head_a, xa, xb)

    if n_seq == 1:
        st = _mm_state(lhs2, states[0])
        a_h, r_h = st[:CHUNK], st[CHUNK:]
    else:
        a_hs, r_hs = [], []
        for s in range(n_seq):
            sl = slice(s * seq_len, (s + 1) * seq_len)
            st = _mm_state(jnp.concatenate([a_t[sl], r_t[sl]], axis=0), states[s])
            a_hs.append(st[:seq_len])
            r_hs.append(st[seq_len:])
        a_h = jnp.concatenate(a_hs, axis=0)
        r_h = jnp.concatenate(r_hs, axis=0)

    rhs = a_h + sel(_mm_tok(parts[0][1], v), _mm_tok(parts[1][1], v))
    yield
    u = sel(_mm_tok(parts[0][0], rhs), _mm_tok(parts[1][0], rhs))
    yield
    vu = jnp.concatenate([v, u], axis=0)
    y = r_h + sel(
        _mm_tok(jnp.concatenate([parts[0][3], -parts[0][2]], axis=1), vu),
        _mm_tok(jnp.concatenate([parts[1][3], -parts[1][2]], axis=1), vu))

    new_states = []
    for s in range(n_seq):
        sl = slice(s * seq_len, (s + 1) * seq_len)
        upd = _mm_state(jnp.concatenate([k_end[sl], -b_end[sl]], axis=0),
                        jnp.concatenate([v[sl], u[sl]], axis=0), TN)
        p_end_col = jnp.exp(cum_t[:, (s + 1) * seq_len - 1:(s + 1) * seq_len])
        new_states.append(states[s] * p_end_col + jnp.where(same_head, upd, 0.0))
    yield

    mean = _mm2l(y, seg_avg)
    yc = y - mean
    yield
    var = _mm2l(yc * yc, seg_avg)
    yn = yc * lax.rsqrt(var + GN_EPS) * ln_w + ln_b
    return (yn + coef * v) * g, new_states


def _rwkv_kernel(r_ref, k_ref, v_ref, da_ref, g_ref, sr_ref, sk_ref, sv_ref, sda_ref, sg_ref,
                 hb0_ref, p_ref, mu2_ref, wd_ref, wa_ref, wg_ref, o_ref, hb_out_ref, *scratch,
                 n_pairs, seq_log2, first_valid):
    stacked = seq_log2 < 7
    n_long = 1 if stacked else r_ref.shape[0]
    rcol = lax.broadcasted_iota(jnp.int32, (CHUNK, 1), 0)
    if stacked:
        valid = rcol >= 0
        seq_start = (rcol & ((1 << seq_log2) - 1)) == 0
    else:
        hb_ref, carry_ref, carry2_ref = scratch
        c = pl.program_id(1)
        n_c = pl.num_programs(1)
        rglob = c * CHUNK + rcol
        valid = rglob >= first_valid

        @pl.when(c == 0)
        def _():
            for q in range(n_long):
                hb_ref[q * n_pairs:(q + 1) * n_pairs] = hb0_ref[q]
                for j, s_ref in enumerate((sr_ref, sk_ref, sv_ref)):
                    carry_ref[3 * q + j] = jnp.broadcast_to(s_ref[q, 0:1, :], carry_ref.shape[1:])
                for j, s_ref in enumerate((sda_ref, sg_ref)):
                    carry2_ref[2 * q + j] = jnp.broadcast_to(s_ref[q, 0:1, :], carry2_ref.shape[1:])

    P = p_ref[...]
    mu2 = mu2_ref[...]
    mus = (P[0:1], P[1:2], P[2:3], mu2[0:1], mu2[1:2])
    x_refs = (r_ref, k_ref, v_ref, da_ref, g_ref)
    s_refs = (sr_ref, sk_ref, sv_ref, sda_ref, sg_ref)
    us_all = []
    for q in range(n_long):
        us = []
        for j in range(5):
            x = x_refs[j][...] if stacked else x_refs[j][q]
            rolled = pltpu.roll(x, 1, axis=0)
            if stacked:
                prev = jnp.where(seq_start, s_refs[j][...], rolled)
            else:
                cref, jj = (carry_ref, 3 * q + j) if j < 3 else (carry2_ref, 2 * q + j - 3)
                prev = jnp.where(rcol == 0, cref[jj][SUB - 1:SUB, :], rolled)
                prev = jnp.where(rglob == first_valid, s_refs[j][q, 0:1, :], prev)
                cref[jj] = x[CHUNK - SUB:CHUNK, :]
            us.append(x + (prev - x) * mus[j])
        us_all.append(us)

    cat = lambda xs: xs[0] if len(xs) == 1 else jnp.concatenate(xs, axis=0)
    u_da = cat([us[3] for us in us_all])
    u_g = cat([us[4] for us in us_all])
    valid_all = cat([valid] * n_long)
    xw = P[3:4] + _mm(jnp.tanh(u_da), wd_ref[...])
    w_log = -_softplus(-xw) - 0.5
    lw = jnp.where(valid_all, -jnp.exp(w_log), 0.0)
    a = jax.nn.sigmoid(P[4:5] + _mm(u_da, wa_ref[...]))
    g = _mm(jax.nn.sigmoid(u_g), wg_ref[...])

    gens = []
    for q in range(n_long):
        rs = slice(q * CHUNK, (q + 1) * CHUNK)
        r, k, v = us_all[q][:3]
        for pr in range(n_pairs):
            sl = slice(pr * LANE, (pr + 1) * LANE)
            prm = tuple(P[i:i + 1, sl] for i in range(5, 10))
            if stacked:
                states = [hb0_ref[s, 0] for s in range(CHUNK >> seq_log2)]
            else:
                states = [hb_ref[q * n_pairs + pr]]
            gens.append(_rwkv_pair_steps(r[:, sl], k[:, sl], v[:, sl], lw[rs, sl], a[rs, sl], g[rs, sl], prm,
                                         valid, states, seq_log2))
    for idx, (out, new_states) in enumerate(_run_interleaved(gens)):
        q, pr = divmod(idx, n_pairs)
        sl = slice(pr * LANE, (pr + 1) * LANE)
        if stacked:
            o_ref[:, sl] = out
            for s, st in enumerate(new_states):
                hb_out_ref[s, 0] = st
        else:
            o_ref[q, :, sl] = out
            hb_ref[idx] = new_states[0]

    if not stacked:
        @pl.when(c == n_c - 1)
        def _():
            for q in range(n_long):
                hb_out_ref[q] = hb_ref[q * n_pairs:(q + 1) * n_pairs]


def rwkv(proj, shift_rows, hb0, params, mu2, wd, wa, wg, seq_len, first_valid):
    n, _ = proj.shape
    n_pr = H_B // 2
    stacked = seq_len < CHUNK
    if stacked:
        assert CHUNK % seq_len == 0 and n % CHUNK == 0 and seq_len & (seq_len - 1) == 0
        seq_log2 = seq_len.bit_length() - 1
        n_pairs, per_step = 1, CHUNK // seq_len
        grid = (n // CHUNK, n_pr)
        rows = lambda g, pr: g
        pcol = lambda g, pr: pr
        hb_block = (per_step, 1, CHUNK, CHUNK)
        hb_map = lambda g, pr: (g, pr, 0, 0)
        s_rows = CHUNK
        srow = lambda g, pr: (g,)
        scratch = []
    else:
        assert seq_len % CHUNK == 0 and n % seq_len == 0
        seq_log2 = 7
        n_c = seq_len // CHUNK
        n_pairs = n_pr
        n_seqs = n // seq_len
        n_long = next(d for d in (4, 2, 1) if n_seqs % d == 0)
        proj = proj.reshape(n_seqs, seq_len, proj.shape[1])
        grid = (n_seqs // n_long, n_c)
        pcol = lambda bi, c: 0
        hb_block = (n_long, n_pr, CHUNK, CHUNK)
        hb_map = lambda bi, c: (bi, 0, 0, 0)
        scratch = [pltpu.VMEM((n_long * n_pr, CHUNK, CHUNK), F32), pltpu.VMEM((3 * n_long, SUB, W_B), F32),
                   pltpu.VMEM((2 * n_long, SUB, LANE), F32)]
    w = n_pairs * LANE
    q0 = 3 * W_A

    def x_spec(col0, width, per_pair):
        blk0 = col0 // width
        if stacked:
            return pl.BlockSpec((CHUNK, width), lambda i, j: (rows(i, j), blk0 + (pcol(i, j) if per_pair else 0)))
        return pl.BlockSpec((n_long, CHUNK, width), lambda i, j: (i, j, blk0))

    def s_spec(col0, width, per_pair):
        blk0 = col0 // width
        if stacked:
            return pl.BlockSpec((s_rows, width), lambda i, j: srow(i, j) + (blk0 + (pcol(i, j) if per_pair else 0),))
        return pl.BlockSpec((n_long, SUB, width), lambda i, j: (i, 0, blk0))

    cols = [(0, w, True), (W_B, w, True), (2 * W_B, w, True), (3 * W_B, LANE, False), (3 * W_B + LANE, LANE, False)]
    in_specs = ([x_spec(q0 + c0, wd_, pp) for c0, wd_, pp in cols]
                + [s_spec(c0, wd_, pp) for c0, wd_, pp in cols]
                + [pl.BlockSpec(hb_block, hb_map),
                   pl.BlockSpec((16, w), lambda i, j: (0, pcol(i, j))),
                   pl.BlockSpec((SUB, LANE), lambda i, j: (0, 0)),
                   pl.BlockSpec((LANE, w), lambda i, j: (0, pcol(i, j))),
                   pl.BlockSpec((LANE, w), lambda i, j: (0, pcol(i, j))),
                   pl.BlockSpec((LANE, w), lambda i, j: (0, pcol(i, j)))])
    kern = functools.partial(_rwkv_kernel, n_pairs=n_pairs, seq_log2=seq_log2, first_valid=first_valid)
    if stacked:
        o_spec = pl.BlockSpec((CHUNK, w), lambda i, j: (rows(i, j), pcol(i, j)))
        o_shape = jax.ShapeDtypeStruct((n, W_B), F32)
    else:
        o_spec = pl.BlockSpec((n_long, CHUNK, w), lambda i, j: (i, j, 0))
        o_shape = jax.ShapeDtypeStruct((n_seqs, seq_len, W_B), F32)
    out, hb = pl.pallas_call(
        kern,
        grid=grid,
        in_specs=in_specs,
        out_specs=[o_spec, pl.BlockSpec(hb_block, hb_map)],
        out_shape=[o_shape, jax.ShapeDtypeStruct(hb0.shape, F32)],
        scratch_shapes=scratch,
        compiler_params=_cparams(("parallel", "arbitrary")),
        name="rwkv",
    )(proj, proj, proj, proj, proj, shift_rows, shift_rows, shift_rows, shift_rows, shift_rows,
      hb0, params, mu2, wd, wa, wg)
    return out.reshape(n, W_B), hb


def _scan_rows(x, n):
    ridx = lax.broadcasted_iota(jnp.int32, (n, 1), 0)
    s = 1
    while s < n:
        x = x + jnp.where(ridx >= s, pltpu.roll(x, s, axis=0), 0.0)
        s *= 2
    return x


def _scan_lanes(x):
    lidx = lax.broadcasted_iota(jnp.int32, (1, LANE), 1)
    s = 1
    while s < LANE:
        x = x + jnp.where(lidx >= s, pltpu.roll(x, s, axis=1), 0.0)
        s *= 2
    return x


def _mlstm_kernel(q_ref, k_ref, v_ref, og_ref, gc_ref, gt_ref, conv0_ref, c0_ref, n0_ref, m0_ref,
                  cw_ref, cb_ref, gb_row_ref, gb_col_ref, nw_ref,
                  o_ref, c_out_ref, n_out_ref, m_out_ref,
                  cext_ref, m_ref, halo_ref, *, first_valid):
    c = pl.program_id(1)
    n_c = pl.num_programs(1)
    n_seq, lq = q_ref.shape[0], q_ref.shape[1]
    rcol = lax.broadcasted_iota(jnp.int32, (lq, 1), 0)
    rglob = c * lq + rcol
    valid_c = rglob >= first_valid
    lrow = lax.broadcasted_iota(jnp.int32, (1, LANE), 1)
    valid_r = ((c * lq + lrow) >= first_valid) & (lrow < lq)
    causal = lax.broadcasted_iota(jnp.int32, (lq, LANE), 1) <= lax.broadcasted_iota(jnp.int32, (lq, LANE), 0)
    lane_is_f = (lrow >= H_C) & (lrow < 2 * H_C)
    row_is_f = lax.broadcasted_iota(jnp.int32, (SUB, 1), 0) >= H_C
    ones_ext = jnp.ones((lq, LANE), F32)
    cw = cw_ref[...]

    @pl.when(c == 0)
    def _():
        for sq in range(n_seq):
            for h in range(H_C):
                cext_ref[sq * H_C + h, :, :DV_C] = c0_ref[0, sq, h]
                cext_ref[sq * H_C + h, :, DV_C:] = n0_ref[sq, h]
            m_ref[sq * H_C:(sq + 1) * H_C] = m0_ref[sq]
            halo_ref[sq] = conv0_ref[sq]

    def seq_terms(sq):
        x = jnp.where(valid_c, jnp.concatenate([q_ref[sq], k_ref[sq]], axis=1), 0.0)
        xcat = jnp.concatenate([halo_ref[sq], x], axis=0)
        conv = cb_ref[...] + x * cw[CONV_W - 1:CONV_W]
        for dlt in range(1, CONV_W):
            conv = conv + pltpu.roll(xcat, dlt, axis=0)[SUB:SUB + lq] * cw[CONV_W - 1 - dlt:CONV_W - dlt]
        halo_ref[sq] = xcat[lq:lq + SUB]
        qk = conv * jax.nn.sigmoid(conv)

        gcol = gc_ref[sq] + gb_row_ref[...]
        gcol = jnp.where(lane_is_f, -_softplus(-gcol), gcol)
        gcol = jnp.where(valid_c, gcol, jnp.where(lane_is_f, 0.0, -jnp.inf))
        bcum_cols = _scan_rows(jnp.where(lane_is_f, gcol, 0.0), lq)
        grow = gt_ref[sq] + gb_col_ref[...]
        grow = jnp.where(row_is_f, -_softplus(-grow), grow)
        grow = jnp.where(valid_r, grow, jnp.where(row_is_f, 0.0, -jnp.inf))
        bcum_rows = _scan_lanes(jnp.where(row_is_f, grow, 0.0))
        return qk, gcol, bcum_cols, grow, bcum_rows

    def head_steps(sq, h, qk, gcol, bcum_cols, grow, bcum_rows):
        slot = sq * H_C + h
        q = qk[:, h * DK_C:(h + 1) * DK_C]
        k = qk[:, QK_C + h * DK_C:QK_C + (h + 1) * DK_C] * (DK_C ** -0.5)
        v_ext = jnp.concatenate([v_ref[sq, :, h * DV_C:(h + 1) * DV_C], ones_ext], axis=1)
        li_c = gcol[:, h:h + 1]
        bc_c = bcum_cols[:, H_C + h:H_C + h + 1]
        li_r = grow[h:h + 1, :]
        bc_r = bcum_rows[H_C + h:H_C + h + 1, :]
        m_prev = m_ref[slot, 0:1, 0:1]
        cext = cext_ref[slot]

        dmat = jnp.where(causal, bc_c - bc_r + li_r, -jnp.inf)
        inter = bc_c + m_prev
        m_row = jnp.maximum(inter, jnp.max(dmat, axis=1, keepdims=True))
        k_pad = _pad_rows(k, CHUNK)
        s = _mm(q, k_pad, NT)
        q_c = _mm(q, cext)
        yield
        w_intra = jnp.exp(dmat - m_row)
        w_inter = jnp.exp(inter - m_row)
        s = s * w_intra
        yield
        num_ext = _mm(s, _pad_rows(v_ext, CHUNK)) + w_inter * q_c
        den = num_ext[:, DV_C:DV_C + 1]
        hh = num_ext[:, :DV_C] / jnp.maximum(jnp.abs(den), jnp.exp(-m_row))

        m_new = m_row[lq - 1:lq, :]
        bc_last = bc_c[lq - 1:lq, :]
        w_end = jnp.exp(bc_last - bc_c + li_c - m_new)
        dec = jnp.exp(bc_last + m_prev - m_new)
        cext_ref[slot] = dec * cext + _mm(k_pad, _pad_rows(w_end * v_ext, CHUNK), TN)
        m_ref[slot] = jnp.broadcast_to(m_new, (SUB, LANE))
        yield

        mu = jnp.mean(hh, axis=1, keepdims=True)
        hc = hh - mu
        yield
        var = jnp.mean(hc * hc, axis=1, keepdims=True)
        hn = hc * lax.rsqrt(var + RMS_EPS) * nw_ref[:, h * DV_C:(h + 1) * DV_C]
        o_ref[sq, :, h * DV_C:(h + 1) * DV_C] = hn * jax.nn.sigmoid(og_ref[sq, :, h * DV_C:(h + 1) * DV_C])

    terms = [seq_terms(sq) for sq in range(n_seq)]
    _run_interleaved([head_steps(sq, h, *terms[sq]) for sq in range(n_seq) for h in range(H_C)])

    @pl.when(c == n_c - 1)
    def _():
        for sq in range(n_seq):
            for h in range(H_C):
                c_out_ref[sq, h] = cext_ref[sq * H_C + h, :, :DV_C]
                n_out_ref[sq, h] = cext_ref[sq * H_C + h, :, DV_C:]
            m_out_ref[sq] = m_ref[sq * H_C:(sq + 1) * H_C]


def mlstm(proj, gates_t, conv0, c0_all, layer, n0b, m0b, conv_w, conv_b, gb_row, gb_col, norm_w, first_valid):
    bsz, t, _ = proj.shape
    lq = min(t, CHUNK)
    n_c = t // lq
    assert t % lq == 0 and gates_t.shape[2] == n_c * CHUNK
    qk2 = 2 * QK_C
    nq = next(d for d in (4, 2, 1) if bsz % d == 0)
    kern = functools.partial(_mlstm_kernel, first_valid=first_valid)
    const2 = lambda bi, c: (0, 0)
    return pl.pallas_call(
        kern,
        grid=(bsz // nq, n_c),
        in_specs=[pl.BlockSpec((nq, lq, QK_C), lambda bi, c: (bi, c, 0)),
                  pl.BlockSpec((nq, lq, QK_C), lambda bi, c: (bi, c, 1)),
                  pl.BlockSpec((nq, lq, V_C), lambda bi, c: (bi, c, qk2 // V_C)),
                  pl.BlockSpec((nq, lq, V_C), lambda bi, c: (bi, c, qk2 // V_C + 1)),
                  pl.BlockSpec((nq, lq, LANE), lambda bi, c: (bi, c, (qk2 + 2 * V_C) // LANE)),
                  pl.BlockSpec((nq, SUB, LANE), lambda bi, c: (bi, 0, c)),
                  pl.BlockSpec((nq, SUB, qk2), lambda bi, c: (bi, 0, 0)),
                  pl.BlockSpec((1, nq, H_C, DK_C, DV_C), lambda bi, c: (layer, bi, 0, 0, 0)),
                  pl.BlockSpec((nq, H_C, DK_C, LANE), lambda bi, c: (bi, 0, 0, 0)),
                  pl.BlockSpec((nq, H_C, SUB, LANE), lambda bi, c: (bi, 0, 0, 0)),
                  pl.BlockSpec((SUB, qk2), const2),
                  pl.BlockSpec((1, qk2), const2),
                  pl.BlockSpec((1, LANE), const2),
                  pl.BlockSpec((SUB, LANE), const2),
                  pl.BlockSpec((1, V_C), const2)],
        out_specs=[pl.BlockSpec((nq, lq, V_C), lambda bi, c: (bi, c, 0)),
                   pl.BlockSpec((nq, H_C, DK_C, DV_C), lambda bi, c: (bi, 0, 0, 0)),
                   pl.BlockSpec((nq, H_C, DK_C, LANE), lambda bi, c: (bi, 0, 0, 0)),
                   pl.BlockSpec((nq, H_C, SUB, LANE), lambda bi, c: (bi, 0, 0, 0))],
        out_shape=[jax.ShapeDtypeStruct((bsz, t, V_C), F32),
                   jax.ShapeDtypeStruct((bsz, H_C, DK_C, DV_C), F32),
                   jax.ShapeDtypeStruct((bsz, H_C, DK_C, LANE), F32),
                   jax.ShapeDtypeStruct((bsz, H_C, SUB, LANE), F32)],
        scratch_shapes=[pltpu.VMEM((nq * H_C, DK_C, DV_C + LANE), F32),
                        pltpu.VMEM((nq * H_C, SUB, LANE), F32),
                        pltpu.VMEM((nq, SUB, qk2), F32)],
        compiler_params=_cparams(("parallel", "arbitrary")),
        name="mlstm",
    )(proj, proj, proj, proj, proj, gates_t, conv0, c0_all, n0b, m0b, conv_w, conv_b, gb_row, gb_col, norm_w)


def _pad_to(x, axis, size):
    pad = [(0, 0)] * x.ndim
    pad[axis] = (0, size - x.shape[axis])
    return jnp.pad(x, pad)


def _rwkv_params(i, rwkv_mu, rwkv_w0, rwkv_a0, rwkv_k_k, rwkv_k_a, rwkv_r_k, rwkv_ln_w, rwkv_ln_b,
                 w_decay_up, w_iclr_up, w_gate_up):
    mu = rwkv_mu[i]
    rows = [mu[0:W_B], mu[W_B:2 * W_B], mu[2 * W_B:3 * W_B], rwkv_w0[i], rwkv_a0[i], rwkv_k_k[i],
            rwkv_k_a[i], rwkv_r_k[i], rwkv_ln_w[i], rwkv_ln_b[i]]
    pp = _pad_to(jnp.stack(rows, axis=0), 0, 16)
    mu2 = _pad_to(mu[3 * W_B:].reshape(2, LANE), 0, SUB)
    wd = jnp.concatenate([w_decay_up[i], jnp.zeros((R_ICLR, W_B), F32)], axis=0).astype(BF16)
    wa = jnp.concatenate([jnp.zeros((R_DECAY, W_B), F32), w_iclr_up[i]], axis=0).astype(BF16)
    wg = w_gate_up[i].astype(BF16)
    return pp, mu2, wd, wa, wg


def _pair_states(wkv):
    bsz = wkv.shape[0]
    st = jnp.swapaxes(wkv, -1, -2).reshape(bsz, H_B // 2, 2, DH_B, DH_B)
    z = jnp.zeros_like(st[:, :, 0])
    top = jnp.concatenate([st[:, :, 0], z], axis=-1)
    bot = jnp.concatenate([z, st[:, :, 1]], axis=-1)
    return jnp.concatenate([top, bot], axis=-2)


def _unpair_states(hb):
    bsz = hb.shape[0]
    a = hb[:, :, :DH_B, :DH_B]
    b = hb[:, :, DH_B:, DH_B:]
    st = jnp.stack([a, b], axis=2).reshape(bsz, H_B, DH_B, DH_B)
    return jnp.swapaxes(st, -1, -2)


def kernel(x_prompt, x_sample, cache_k, cache_v, page_table, state_wkv, state_shift, state_mlstm_c, state_mlstm_n, state_mlstm_m, state_conv, meta_tokens, g_mix_pre, g_mix_post, g_ffn_pre, g_ffn_post, w_in_ab, w_out_ab, sb_bias, rwkv_mu, rwkv_w0, rwkv_w_decay_up, rwkv_a0, rwkv_w_iclr_up, rwkv_w_gate_up, rwkv_k_k, rwkv_k_a, rwkv_r_k, rwkv_ln_w, rwkv_ln_b, w_in_c, conv_w, conv_b, mlstm_b_i, mlstm_b_f, mlstm_norm_w, w_out_c, w_ffn_gate, w_ffn_up, w_ffn_down):
    bsz, seq, d = x_prompt.shape
    db, ds, _ = x_sample.shape
    depth = g_mix_pre.shape[0]
    tp = PAD_FRONT + N_META + seq
    assert seq % CHUNK == 0 and ds % SUB == 0 and ds <= CHUNK and seq >= CONV_W and ds >= CONV_W

    meta = jnp.broadcast_to(meta_tokens.astype(F32)[None], (bsz, N_META, d))
    xp = jnp.concatenate([jnp.zeros((bsz, PAD_FRONT, d), F32), meta, x_prompt], axis=1).reshape(bsz * tp, d)
    xs = x_sample.reshape(db * ds, d)

    n_layers_ab, n_pool = cache_k.shape[:2]
    pool_k = jnp.transpose(cache_k, (0, 1, 3, 4, 2)).reshape(n_layers_ab, n_pool, W_A, CHUNK)
    pool_v = jnp.transpose(cache_v, (0, 1, 3, 4, 2)).reshape(n_layers_ab, n_pool, W_A, CHUNK)
    outs = {name: [] for name in ("k_p", "v_p", "wkv_p", "sh_p", "c_p", "n_p", "m_p", "cv_p",
                                  "k_s", "v_s", "wkv_s", "sh_s", "c_s", "n_s", "m_s", "cv_s")}
    row = lambda v: v.reshape(1, -1)

    for layer in range(depth):
        i = layer // 2
        if layer % 2 == 0:
            w_in = w_in_ab[i].astype(BF16)
            w_out = w_out_ab[i].astype(BF16)
            pp, mu2, wd, wa, wg = _rwkv_params(i, rwkv_mu, rwkv_w0, rwkv_a0, rwkv_k_k, rwkv_k_a, rwkv_r_k,
                                               rwkv_ln_w, rwkv_ln_b, rwkv_w_decay_up, rwkv_w_iclr_up, rwkv_w_gate_up)
            new_x = []
            for grp, x in (("p", xp), ("s", xs)):
                nb, t = (bsz, tp) if grp == "p" else (db, ds)
                proj = norm_matmul(x, row(g_mix_pre[layer]), w_in).reshape(nb, t, AB_COLS)
                if grp == "p":
                    o_a = sb_prompt(proj, sb_bias[i], PAD_FRONT)
                    shift_rows = jnp.zeros((nb, SUB, B_COLS), F32)
                    hb0 = jnp.zeros((nb, H_B // 2, CHUNK, CHUNK), F32)
                    first_valid = PAD_FRONT
                else:
                    o_a = sb_sample(proj, pool_k, pool_v, i, page_table, sb_bias[i])
                    shift_rows = _pad_to(state_shift[i][:, None, :], 1, t).reshape(nb * t, B_COLS)
                    hb0 = _pair_states(state_wkv[i])
                    first_valid = 0
                o_b, hb = rwkv(proj.reshape(nb * t, AB_COLS), shift_rows, hb0, pp, mu2, wd, wa, wg, t, first_valid)
                new_x.append(proj_res([o_a.reshape(nb * t, W_A), o_b.reshape(nb * t, W_B)],
                                      [w_out[:W_A], w_out[W_A:]], row(g_mix_post[layer]), x))
                outs["k_" + grp].append(proj[:, first_valid:, W_A:2 * W_A].reshape(nb, t - first_valid, H_A, DH_A))
                outs["v_" + grp].append(proj[:, first_valid:, 2 * W_A:3 * W_A].reshape(nb, t - first_valid, H_A, DH_A))
                outs["wkv_" + grp].append(_unpair_states(hb))
                outs["sh_" + grp].append(proj[:, t - 1, 3 * W_A:])
            xp, xs = new_x
        else:
            wc = w_in_c[i]
            ncol = wc.shape[1]
            wperm = jnp.concatenate([wc[:, :2 * QK_C + V_C], wc[:, ncol - V_C:],
                                     wc[:, 2 * QK_C + V_C:ncol - V_C]], axis=1)
            wperm = _pad_to(wperm, 1, C_COLS_PAD).astype(BF16)
            w_out = w_out_c[i].astype(BF16)
            cw = _pad_to(conv_w[i], 0, SUB)
            gb = jnp.concatenate([mlstm_b_i[i], mlstm_b_f[i]])
            gb_row = _pad_to(gb, 0, LANE).reshape(1, LANE)
            gb_col = jnp.broadcast_to(gb[:, None], (SUB, LANE))
            new_x = []
            for grp, x in (("p", xp), ("s", xs)):
                nb, t = (bsz, tp) if grp == "p" else (db, ds)
                proj = norm_matmul(x, row(g_mix_pre[layer]), wperm).reshape(nb, t, C_COLS_PAD)
                g0 = 2 * QK_C + 2 * V_C
                gates_t = jnp.swapaxes(proj[:, :, g0:g0 + SUB], 1, 2)
                if grp == "p":
                    first_valid = PAD_FRONT
                    conv0 = jnp.zeros((nb, SUB, 2 * QK_C), F32)
                    c0_all, c0_layer = jnp.zeros((1, nb, H_C, DK_C, DV_C), F32), 0
                    n0 = jnp.zeros((nb, H_C, DK_C), F32)
                    m0 = jnp.zeros((nb, H_C), F32)
                else:
                    first_valid = 0
                    gates_t = _pad_to(gates_t, 2, CHUNK)
                    conv0 = jnp.concatenate([jnp.zeros((nb, SUB - (CONV_W - 1), 2 * QK_C), F32), state_conv[i]], axis=1)
                    c0_all, c0_layer = state_mlstm_c, i
                    n0, m0 = state_mlstm_n[i], state_mlstm_m[i]
                n0b = jnp.broadcast_to(n0[..., None], (nb, H_C, DK_C, LANE))
                m0b = jnp.broadcast_to(m0[..., None, None], (nb, H_C, SUB, LANE))
                hmix, c_new, n_new, m_new = mlstm(proj, gates_t, conv0, c0_all, c0_layer, n0b, m0b, cw,
                                                  row(conv_b[i]), gb_row, gb_col, row(mlstm_norm_w[i]), first_valid)
                new_x.append(proj_res([hmix.reshape(nb * t, V_C)], [w_out], row(g_mix_post[layer]), x))
                outs["c_" + grp].append(c_new)
                outs["n_" + grp].append(n_new[..., 0])
                outs["m_" + grp].append(m_new[..., 0, 0])
                outs["cv_" + grp].append(proj[:, t - (CONV_W - 1):, :2 * QK_C])
            xp, xs = new_x
        wgt, wup, wdn = (w_ffn_gate[layer].astype(BF16), w_ffn_up[layer].astype(BF16),
                         w_ffn_down[layer].astype(BF16))
        xp = ffn(xp, row(g_ffn_pre[layer]), wgt, wup, wdn, row(g_ffn_post[layer]))
        xs = ffn(xs, row(g_ffn_pre[layer]), wgt, wup, wdn, row(g_ffn_post[layer]))

    y_prompt = xp.reshape(bsz, tp, d)[:, PAD_FRONT + N_META:]
    y_sample = xs.reshape(db, ds, d)
    st = lambda name: jnp.stack(outs[name])
    return (y_prompt, y_sample, st("k_p"), st("v_p"), st("wkv_p"), st("sh_p"), st("c_p"), st("n_p"),
            st("m_p"), st("cv_p"), st("k_s"), st("v_s"), st("wkv_s"), st("sh_s"), st("c_s"), st("n_s"),
            st("m_s"), st("cv_s"))
```

```python
import functools

import jax
import jax.numpy as jnp
from jax import lax
from jax.experimental import pallas as pl
from jax.experimental.pallas import tpu as pltpu

F32 = jnp.float32
BF16 = jnp.bfloat16

D_MODEL = 1024
N_META = 16
H_A, DH_A = 8, 64
W_A = H_A * DH_A
H_B, DH_B = 8, 64
W_B = H_B * DH_B
R_DECAY, R_ICLR, R_GATE = 64, 64, 128
B_COLS = 3 * W_B + R_DECAY + R_ICLR + R_GATE
AB_COLS = 3 * W_A + B_COLS
H_C, DK_C, DV_C = 4, 128, 256
QK_C = H_C * DK_C
V_C = H_C * DV_C
CONV_W = 4
C_COLS_PAD = 2 * QK_C + 2 * V_C + 128
RMS_EPS = 1e-6
GN_EPS = 64e-5
LOG2E = 1.4426950408889634

LANE = 128
SUB = 8
CHUNK = 128
PAD_FRONT = CHUNK - N_META
VMEM_LIMIT = 56 * 1024 * 1024

NN = (((1,), (0,)), ((), ()))
NT = (((1,), (1,)), ((), ()))
TN = (((0,), (0,)), ((), ()))


def _cparams(sem):
    return pltpu.CompilerParams(dimension_semantics=sem, vmem_limit_bytes=VMEM_LIMIT)


def _dg(a, b, dims=NN):
    return lax.dot_general(a, b, dims, preferred_element_type=F32)


def _split(x):
    hi = x.astype(BF16)
    lo = (x - hi.astype(F32)).astype(BF16)
    return hi, lo


def _mm(a, b, dims=NN):
    return _dg(a.astype(BF16), b.astype(BF16), dims)


def _mm3(a, b, dims=NN):
    ah, al = _split(a)
    bh, bl = _split(b)
    return _dg(ah, bh, dims) + (_dg(ah, bl, dims) + _dg(al, bh, dims))


def _mm2l(a, b_bf16, dims=NN):
    ah, al = _split(a)
    return _dg(ah, b_bf16, dims) + _dg(al, b_bf16, dims)


_mm_tok = _mm
_mm_state = _mm


def _rms(x, g):
    ms = jnp.mean(x * x, axis=-1, keepdims=True)
    return x * lax.rsqrt(ms + RMS_EPS) * g


def _softplus(z):
    return jnp.maximum(z, 0.0) + jnp.log1p(jnp.exp(-jnp.abs(z)))


def _pick_tm(n, cap=512):
    for tm in (512, 384, 256, 128, 64, 32, 16, 8):
        if tm <= cap and n % tm == 0:
            return tm
    raise ValueError(f"row count {n} not a multiple of 8")


def _pad_rows(x, rows):
    if x.shape[0] == rows:
        return x
    return jnp.concatenate([x, jnp.zeros((rows - x.shape[0],) + x.shape[1:], x.dtype)], axis=0)


def _norm_matmul_kernel(x_ref, g_ref, w_ref, o_ref):
    h = _rms(x_ref[...], g_ref[...]).astype(BF16)
    o_ref[...] = _dg(h, w_ref[0])


def norm_matmul(x, g, w_all, layer):
    n, d = x.shape
    c = w_all.shape[2]
    tm = _pick_tm(n)
    return pl.pallas_call(
        _norm_matmul_kernel,
        grid=(n // tm,),
        in_specs=[pl.BlockSpec((tm, d), lambda i: (i, 0)),
                  pl.BlockSpec((1, d), lambda i: (0, 0)),
                  pl.BlockSpec((1, d, c), lambda i: (layer, 0, 0))],
        out_specs=pl.BlockSpec((tm, c), lambda i: (i, 0)),
        out_shape=jax.ShapeDtypeStruct((n, c), F32),
        compiler_params=_cparams(("parallel",)),
        name="norm_matmul",
    )(x, g, w_all)


def _proj_res_kernel(*refs, n_in):
    ins, ws = refs[:n_in], refs[n_in:2 * n_in]
    g_ref, x_ref, o_ref = refs[2 * n_in:]
    acc = _dg(ins[0][...].astype(BF16), ws[0][0])
    for a, w in zip(ins[1:], ws[1:]):
        acc = acc + _dg(a[...].astype(BF16), w[0])
    o_ref[...] = x_ref[...] + _rms(acc, g_ref[...])


def proj_res(ins, w_all, layer, g, x):
    n, d = x.shape
    tm = _pick_tm(n)
    n_in = len(ins)
    kw = ins[0].shape[1]
    assert all(a.shape[1] == kw for a in ins) and w_all.shape[1] == n_in * kw
    ws = [w_all] * n_in
    in_specs = ([pl.BlockSpec((tm, kw), lambda i: (i, 0)) for _ in ins]
                + [pl.BlockSpec((1, kw, d), lambda i, j=j: (layer, j, 0)) for j in range(n_in)]
                + [pl.BlockSpec((1, d), lambda i: (0, 0)), pl.BlockSpec((tm, d), lambda i: (i, 0))])
    return pl.pallas_call(
        functools.partial(_proj_res_kernel, n_in=n_in),
        grid=(n // tm,),
        in_specs=in_specs,
        out_specs=pl.BlockSpec((tm, d), lambda i: (i, 0)),
        out_shape=jax.ShapeDtypeStruct((n, d), F32),
        compiler_params=_cparams(("parallel",)),
        name="proj_res",
    )(*ins, *ws, g, x)


def _ffn_kernel(x_ref, gpre_ref, wg_ref, wu_ref, wd_ref, gpost_ref, o_ref, *, n_split):
    x = x_ref[...]
    h = _rms(x, gpre_ref[...]).astype(BF16)
    d_ff = wg_ref.shape[2]
    step = d_ff // n_split
    ff = None
    for s in range(n_split):
        gate = _dg(h, wg_ref[0, :, s * step:(s + 1) * step])
        up = _dg(h, wu_ref[0, :, s * step:(s + 1) * step])
        a = (gate * jax.nn.sigmoid(gate) * up).astype(BF16)
        part = _dg(a, wd_ref[0, s * step:(s + 1) * step, :])
        ff = part if ff is None else ff + part
    o_ref[...] = x + _rms(ff, gpost_ref[...])


def ffn(x, gpre, wg, wu, wd, layer, gpost):
    n, d = x.shape
    d_ff = wg.shape[2]
    tm = _pick_tm(n, 256)
    n_split = 2 if d_ff % (2 * LANE) == 0 else 1
    const = lambda i: (0, 0)
    at_layer = lambda i: (layer, 0, 0)
    return pl.pallas_call(
        functools.partial(_ffn_kernel, n_split=n_split),
        grid=(n // tm,),
        in_specs=[pl.BlockSpec((tm, d), lambda i: (i, 0)),
                  pl.BlockSpec((1, d), const),
                  pl.BlockSpec((1, d, d_ff), at_layer),
                  pl.BlockSpec((1, d, d_ff), at_layer),
                  pl.BlockSpec((1, d_ff, d), at_layer),
                  pl.BlockSpec((1, d), const)],
        out_specs=pl.BlockSpec((tm, d), lambda i: (i, 0)),
        out_shape=jax.ShapeDtypeStruct((n, d), F32),
        compiler_params=_cparams(("parallel",)),
        name="ffn",
    )(x, gpre, wg, wu, wd, gpost)


def _later_strict(n_heads):
    n = n_heads * CHUNK
    j = lax.broadcasted_iota(jnp.int32, (n, n), 0)
    s = lax.broadcasted_iota(jnp.int32, (n, n), 1)
    return jnp.where(((j >> 7) == (s >> 7)) & (j > s), 1.0, 0.0).astype(BF16)


def _sb_terms2(z2, mask):
    neg_abs = lax.bitcast_convert_type(lax.bitcast_convert_type(z2, jnp.uint32) | jnp.uint32(0x80000000), F32)
    sp = jnp.maximum(z2, 0.0) + jnp.log2(1.0 + jnp.exp2(neg_abs))
    lb = z2 - sp
    if mask is not None:
        sp = jnp.where(mask, sp, 0.0)
    return lb, sp


def _sb_prompt_kernel(bias_ref, q_ref, k_ref, v_ref, o_ref, acc_ref, ra_ref, rb_ref, *, first_valid, scale):
    hp = pl.program_id(1)
    qi = pl.program_id(2)
    tq = q_ref.shape[1]
    q = (q_ref[0] * (scale * LOG2E)).astype(BF16)
    head_a = lax.broadcasted_iota(jnp.int32, (1, LANE), 1) < DH_A
    head_a2 = lax.broadcasted_iota(jnp.int32, (1, 2 * CHUNK), 1) < CHUNK
    bias2 = jnp.where(head_a2, bias_ref[2 * hp], bias_ref[2 * hp + 1]) * LOG2E
    later = _later_strict(2)
    acc_ref[...] = jnp.zeros_like(acc_ref)
    ra_ref[...] = jnp.zeros_like(ra_ref)
    rb_ref[...] = jnp.zeros_like(rb_ref)

    n_grp = tq // CHUNK

    def process_group(j0, masked):
        kjs = [j0 + g for g in reversed(range(n_grp))]
        starts = [pl.multiple_of(kj * CHUNK, CHUNK) for kj in kjs]
        masks = [None] * n_grp
        if masked:
            t_pos = qi * tq + lax.broadcasted_iota(jnp.int32, (tq, 2 * CHUNK), 0)
            s_loc = lax.broadcasted_iota(jnp.int32, (tq, 2 * CHUNK), 1) & (CHUNK - 1)
            masks = [kj * CHUNK + s_loc < t_pos for kj in kjs]
        z2s = []
        for start in starts:
            k = k_ref[0, pl.ds(start, CHUNK), :]
            k2 = jnp.concatenate([jnp.where(head_a, k, 0.0), jnp.where(head_a, 0.0, k)], axis=0).astype(BF16)
            z2s.append(_dg(q, k2, NT))
        tiles = []
        for z2, mask in zip(z2s, masks):
            lb, sp = _sb_terms2(z2 + bias2, mask)
            within = _dg(sp.astype(BF16), later)
            rs_a = jnp.sum(sp[:, :CHUNK], axis=1, keepdims=True)
            rs_b = jnp.sum(sp[:, CHUNK:], axis=1, keepdims=True)
            tiles.append((lb - within, rs_a, rs_b))
        ra, rb, acc = ra_ref[...], rb_ref[...], acc_ref[...]
        for (ex, rs_a, rs_b), mask, kj, start in zip(tiles, masks, kjs, starts):
            att = jnp.exp2(jnp.concatenate([ex[:, :CHUNK] - ra, ex[:, CHUNK:] - rb], axis=1))
            if mask is not None:
                att = jnp.where(mask, att, 0.0)
            krow = kj * CHUNK + lax.broadcasted_iota(jnp.int32, (CHUNK, 1), 0)
            v = jnp.where(krow >= first_valid, v_ref[0, pl.ds(start, CHUNK), :], 0.0)
            v2 = jnp.concatenate([jnp.where(head_a, v, 0.0), jnp.where(head_a, 0.0, v)], axis=0).astype(BF16)
            acc = acc + _dg(att.astype(BF16), v2)
            ra = ra + rs_a
            rb = rb + rs_b
        ra_ref[...], rb_ref[...], acc_ref[...] = ra, rb, acc

    process_group(qi * n_grp, True)

    def body(i, carry):
        process_group((qi - 1 - i) * n_grp, False)
        return carry

    lax.fori_loop(0, qi, body, 0)
    o_ref[0] = acc_ref[...]


def sb_prompt(proj, bias, first_valid):
    b, t, _ = proj.shape
    tq = 3 * CHUNK if t % (3 * CHUNK) == 0 else CHUNK
    n_hp = H_A // 2
    kern = functools.partial(_sb_prompt_kernel, first_valid=first_valid, scale=DH_A ** -0.5)
    return pl.pallas_call(
        kern,
        grid=(b, n_hp, t // tq),
        in_specs=[pl.BlockSpec(memory_space=pltpu.SMEM),
                  pl.BlockSpec((1, tq, LANE), lambda bi, h, qi: (bi, qi, h)),
                  pl.BlockSpec((1, t, LANE), lambda bi, h, qi: (bi, 0, n_hp + h)),
                  pl.BlockSpec((1, t, LANE), lambda bi, h, qi: (bi, 0, 2 * n_hp + h))],
        out_specs=pl.BlockSpec((1, tq, LANE), lambda bi, h, qi: (bi, qi, h)),
        out_shape=jax.ShapeDtypeStruct((b, t, W_A), F32),
        scratch_shapes=[pltpu.VMEM((tq, LANE), F32)] * 3,
        compiler_params=_cparams(("parallel", "parallel", "arbitrary")),
        name="sb_prompt",
    )(bias, proj, proj, proj)


def _sb_sample_kernel(pt_ref, q_ref, kn_ref, vn_ref, bias_ref, *rest, scale, n_g):
    kp_refs, vp_refs = rest[:n_g], rest[n_g:2 * n_g]
    o_ref, qbd_ref, acc_ref, r_ref = rest[2 * n_g:]
    p = pl.program_id(1)
    n_steps = pl.num_programs(1)
    tq = q_ref.shape[1]
    rows = H_A * tq
    later = _later_strict(1)
    col_head = lax.broadcasted_iota(jnp.int32, (1, W_A), 1) >> 6
    bias2 = bias_ref[...] * LOG2E

    def scores(z2s, mask):
        terms = [_sb_terms2(z2 + bias2, mask) for z2 in z2s]
        withins = [_dg(sp.astype(BF16), later) for _, sp in terms]
        return [(lb - within, jnp.sum(sp, axis=1, keepdims=True)) for (lb, sp), within in zip(terms, withins)]

    @pl.when(p == 0)
    def _():
        q = q_ref[0] * (scale * LOG2E)
        qbd_ref[...] = jnp.concatenate(
            [jnp.where(col_head == h, q, 0.0) for h in range(H_A)], axis=0).astype(BF16)
        kn = _pad_rows(kn_ref[0], CHUNK).astype(BF16)
        vn = _pad_rows(vn_ref[0], CHUNK).astype(BF16)
        t_pos = lax.broadcasted_iota(jnp.int32, (rows, CHUNK), 0) & (tq - 1)
        s_pos = lax.broadcasted_iota(jnp.int32, (rows, CHUNK), 1)
        mask = s_pos < t_pos
        (ex, rs), = scores([_dg(qbd_ref[...], kn, NT)], mask)
        att = jnp.where(mask, jnp.exp2(ex), 0.0)
        acc_ref[...] = _dg(att.astype(BF16), vn)
        r_ref[...] = jnp.broadcast_to(rs, r_ref.shape)

    qbd = qbd_ref[...]
    parts = scores([_dg(qbd, kp_refs[j][0, 0].astype(BF16)) for j in range(n_g)], None)
    r = r_ref[...]
    atts = []
    for ex, rs in parts:
        atts.append(jnp.exp2(ex - r).astype(BF16))
        r = r + rs
    acc = acc_ref[...]
    for j in range(n_g):
        acc = acc + _dg(atts[j], vp_refs[j][0, 0].astype(BF16), NT)
    acc_ref[...] = acc
    r_ref[...] = r

    @pl.when(p == n_steps - 1)
    def _():
        out = jnp.where(col_head == 0, acc[0:tq], 0.0)
        for h in range(1, H_A):
            out = out + jnp.where(col_head == h, acc[h * tq:(h + 1) * tq], 0.0)
        o_ref[0] = out


def sb_sample(proj, pool_k, pool_v, layer, page_table, bias):
    db, tq, _ = proj.shape
    n_pages = page_table.shape[1]
    page = pool_k.shape[3]
    assert page == CHUNK and tq % SUB == 0 and tq <= CHUNK and tq & (tq - 1) == 0 and DH_A == 64 and DH_B == 64
    n_g = next(g for g in (16, 8, 4, 2, 1) if n_pages % g == 0)
    rows = H_A * tq
    bias_rows = jnp.broadcast_to(jnp.repeat(bias, tq)[:, None], (rows, CHUNK)).astype(F32)
    pt = page_table.reshape(-1).astype(jnp.int32)

    def page_spec(j):
        return pl.BlockSpec((1, 1, W_A, page), lambda bi, p, pt_ref: (
            layer, pt_ref[bi * n_pages + (n_pages - 1 - (p * n_g + j))], 0, 0))

    grid_spec = pltpu.PrefetchScalarGridSpec(
        num_scalar_prefetch=1,
        grid=(db, n_pages // n_g),
        in_specs=([pl.BlockSpec((1, tq, W_A), lambda bi, p, pt_ref: (bi, 0, 0)),
                   pl.BlockSpec((1, tq, W_A), lambda bi, p, pt_ref: (bi, 0, 1)),
                   pl.BlockSpec((1, tq, W_A), lambda bi, p, pt_ref: (bi, 0, 2)),
                   pl.BlockSpec((rows, CHUNK), lambda bi, p, pt_ref: (0, 0))]
                  + [page_spec(j) for j in range(n_g)] * 2),
        out_specs=pl.BlockSpec((1, tq, W_A), lambda bi, p, pt_ref: (bi, 0, 0)),
        scratch_shapes=[pltpu.VMEM((rows, W_A), BF16),
                        pltpu.VMEM((rows, W_A), F32),
                        pltpu.VMEM((rows, CHUNK), F32)],
    )
    return pl.pallas_call(
        functools.partial(_sb_sample_kernel, scale=DH_A ** -0.5, n_g=n_g),
        grid_spec=grid_spec,
        out_shape=jax.ShapeDtypeStruct((db, tq, W_A), F32),
        compiler_params=_cparams(("parallel", "arbitrary")),
        name="sb_sample",
    )(pt, proj, proj, proj, bias_rows, *([pool_k] * n_g), *([pool_v] * n_g))


def _run_interleaved(gens):
    results = [None] * len(gens)
    active = list(range(len(gens)))
    while active:
        for i in list(active):
            try:
                next(gens[i])
            except StopIteration as stop:
                results[i] = stop.value
                active.remove(i)
    return results


def _unit_lower_inverse_steps(n_mats, row, col, seq_log2):
    eye = jnp.where(row == col, 1.0, 0.0)
    log_base = min(4, seq_log2)
    ds = [jnp.where((row >> log_base) == (col >> log_base), n, 0.0) for n in n_mats]
    xs = [eye - d for d in ds]
    for _ in range(log_base - 1):
        ds = [_mm(d, d) for d in ds]
        yield
        xs = [x + _mm(x, d) for x, d in zip(xs, ds)]
        yield
    for lg in range(log_base, seq_log2):
        off = (row >> (lg + 1)) == (col >> (lg + 1))
        off = jnp.where(off, (row >> lg) & 1, 0) > ((col >> lg) & 1)
        ts = [_mm(jnp.where(off, n, 0.0), x) for n, x in zip(n_mats, xs)]
        yield
        xs = [x - _mm(x, t) for x, t in zip(xs, ts)]
        yield
    return xs


def _rwkv_pair_steps(r, k, v, lw, a, g, prm, valid, states, seq_log2):
    k_k, k_a, r_k, ln_w, ln_b = prm
    seq_len = 1 << seq_log2
    n_seq = CHUNK >> seq_log2
    row = lax.broadcasted_iota(jnp.int32, (CHUNK, CHUNK), 0)
    col = lax.broadcasted_iota(jnp.int32, (CHUNK, CHUNK), 1)
    head_a = lax.broadcasted_iota(jnp.int32, (1, LANE), 1) < DH_B
    same_head = (row >> 6) == (col >> 6)
    same_seq = (row >> seq_log2) == (col >> seq_log2)
    seg_sum = jnp.where(same_head, 1.0, 0.0).astype(BF16)
    seg_avg = jnp.where(same_head, 1.0 / DH_B, 0.0).astype(BF16)

    kk = k * k_k
    kk = kk / jnp.maximum(jnp.sqrt(_mm2l(kk * kk, seg_sum)), 1e-12)
    k_mod = k * (1.0 + (a - 1.0) * k_a)
    coef = _mm2l(r * k_mod * r_k, seg_sum)
    k_mod = jnp.where(valid, k_mod, 0.0)
    b = jnp.where(valid, kk * a, 0.0)

    incl = jnp.where(same_seq & (col <= row), 1.0, 0.0).astype(BF16)
    lw_hi = lw.astype(BF16)
    lw_mid = lw - lw_hi.astype(F32)
    lw_mid_b = lw_mid.astype(BF16)
    lw_lo = (lw_mid - lw_mid_b.astype(F32)).astype(BF16)
    cum = _dg(incl, lw_hi) + (_dg(incl, lw_mid_b) + _dg(incl, lw_lo))
    yield
    ends =[cum[(s + 1) * seq_len - 1:(s + 1) * seq_len, :] for s in range(n_seq)]
    cum_end = ends[0] if n_seq == 1 else jnp.concatenate(
        [jnp.broadcast_to(e, (seq_len, LANE)) for e in ends], axis=0)
    a_t = kk * jnp.exp(cum - lw)
    r_t = r * jnp.exp(cum)
    inv_p = jnp.exp(-cum)
    b_t = b * inv_p
    k_t = k_mod * inv_p
    to_end = jnp.exp(cum_end - cum)
    k_end = k_mod * to_end
    b_end = b * to_end
    cum_t = cum.T

    strict = same_seq & (col < row)
    incl_m = same_seq & (col <= row)
    lhs2 = jnp.concatenate([a_t, r_t], axis=0)
    n_mats, a_aks, a_rbs, a_rks = [], [], [], []
    rhs2 = jnp.concatenate([b_t, k_t], axis=0)
    for head_mask in (head_a, jnp.logical_not(head_a)):
        m = _mm_tok(jnp.where(head_mask, lhs2, 0.0), rhs2, NT)
        m_b, m_k = m[:, :CHUNK], m[:, CHUNK:]
        n_mats.append(jnp.where(strict, m_b[:CHUNK], 0.0))
        a_aks.append(jnp.where(strict, m_k[:CHUNK], 0.0))
        a_rbs.append(jnp.where(incl_m, m_b[CHUNK:], 0.0))
        a_rks.append(jnp.where(incl_m, m_k[CHUNK:], 0.0))
    yield
    t_invs = yield from _unit_lower_inverse_steps(n_mats, row, col, seq_log2)
    parts = list(zip(t_invs, a_aks, a_rbs, a_rks))

    def sel(xa, xb):
        return jnp.where(head_a, xa, xb)

    if n_seq == 1:
        st = _mm_state(lhs2, states[0])
        a_h, r_h = st[:CHUNK], st[CHUNK:]
    else:
        a_hs, r_hs = [], []
        for s in range(n_seq):
            sl = slice(s * seq_len, (s + 1) * seq_len)
            st = _mm_state(jnp.concatenate([a_t[sl], r_t[sl]], axis=0), states[s])
            a_hs.append(st[:seq_len])
            r_hs.append(st[seq_len:])
        a_h = jnp.concatenate(a_hs, axis=0)
        r_h = jnp.concatenate(r_hs, axis=0)

    rhs = a_h + sel(_mm_tok(parts[0][1], v), _mm_tok(parts[1][1], v))
    yield
    u = sel(_mm_tok(parts[0][0], rhs), _mm_tok(parts[1][0], rhs))
    yield
    vu = jnp.concatenate([v, u], axis=0)
    y = r_h + sel(
        _mm_tok(jnp.concatenate([parts[0][3], -parts[0][2]], axis=1), vu),
        _mm_tok(jnp.concatenate([parts[1][3], -parts[1][2]], axis=1), vu))

    new_states = []
    for s in range(n_seq):
        sl = slice(s * seq_len, (s + 1) * seq_len)
        upd = _mm_state(jnp.concatenate([k_end[sl], -b_end[sl]], axis=0),
                        jnp.concatenate([v[sl], u[sl]], axis=0), TN)
        p_end_col = jnp.exp(cum_t[:, (s + 1) * seq_len - 1:(s + 1) * seq_len])
        new_states.append(states[s] * p_end_col + jnp.where(same_head, upd, 0.0))
    yield

    mean = _mm2l(y, seg_avg)
    yc = y - mean
    yield
    var = _mm2l(yc * yc, seg_avg)
    yn = yc * lax.rsqrt(var + GN_EPS) * ln_w + ln_b
    return (yn + coef * v) * g, new_states


def _rwkv_kernel(r_ref, k_ref, v_ref, da_ref, g_ref, sr_ref, sk_ref, sv_ref, sda_ref, sg_ref,
                 hb0_ref, p_ref, mu2_ref, wd_ref, wa_ref, wg_ref, o_ref, hb_out_ref, *scratch,
                 n_pairs, seq_log2, first_valid):
    stacked = seq_log2 < 7
    n_long = 1 if stacked else r_ref.shape[0]
    rcol = lax.broadcasted_iota(jnp.int32, (CHUNK, 1), 0)
    if stacked:
        valid = rcol >= 0
        seq_start = (rcol & ((1 << seq_log2) - 1)) == 0
    else:
        hb_ref, carry_ref, carry2_ref = scratch
        c = pl.program_id(1)
        n_c = pl.num_programs(1)
        rglob = c * CHUNK + rcol
        valid = rglob >= first_valid

        @pl.when(c == 0)
        def _():
            for q in range(n_long):
                hb_ref[q * n_pairs:(q + 1) * n_pairs] = hb0_ref[q]
                for j, s_ref in enumerate((sr_ref, sk_ref, sv_ref)):
                    carry_ref[3 * q + j] = jnp.broadcast_to(s_ref[q, 0:1, :], carry_ref.shape[1:])
                for j, s_ref in enumerate((sda_ref, sg_ref)):
                    carry2_ref[2 * q + j] = jnp.broadcast_to(s_ref[q, 0:1, :], carry2_ref.shape[1:])

    P = p_ref[...]
    mu2 = mu2_ref[...]
    mus = (P[0:1], P[1:2], P[2:3], mu2[0:1], mu2[1:2])
    x_refs = (r_ref, k_ref, v_ref, da_ref, g_ref)
    s_refs = (sr_ref, sk_ref, sv_ref, sda_ref, sg_ref)
    us_all = []
    for q in range(n_long):
        us = []
        for j in range(5):
            x = x_refs[j][...] if stacked else x_refs[j][q]
            rolled = pltpu.roll(x, 1, axis=0)
            if stacked:
                prev = jnp.where(seq_start, s_refs[j][...], rolled)
            else:
                cref, jj = (carry_ref, 3 * q + j) if j < 3 else (carry2_ref, 2 * q + j - 3)
                prev = jnp.where(rcol == 0, cref[jj][SUB - 1:SUB, :], rolled)
                prev = jnp.where(rglob == first_valid, s_refs[j][q, 0:1, :], prev)
                cref[jj] = x[CHUNK - SUB:CHUNK, :]
            us.append(x + (prev - x) * mus[j])
        us_all.append(us)

    cat = lambda xs: xs[0] if len(xs) == 1 else jnp.concatenate(xs, axis=0)
    u_da = cat([us[3] for us in us_all])
    u_g = cat([us[4] for us in us_all])
    valid_all = cat([valid] * n_long)
    xw = P[3:4] + _mm(jnp.tanh(u_da), wd_ref[...])
    w_log = -_softplus(-xw) - 0.5
    lw = jnp.where(valid_all, -jnp.exp(w_log), 0.0)
    a = jax.nn.sigmoid(P[4:5] + _mm(u_da, wa_ref[...]))
    g = _mm(jax.nn.sigmoid(u_g), wg_ref[...])

    gens = []
    for q in range(n_long):
        rs = slice(q * CHUNK, (q + 1) * CHUNK)
        r, k, v = us_all[q][:3]
        for pr in range(n_pairs):
            sl = slice(pr * LANE, (pr + 1) * LANE)
            prm = tuple(P[i:i + 1, sl] for i in range(5, 10))
            if stacked:
                states = [hb0_ref[s, 0] for s in range(CHUNK >> seq_log2)]
            else:
                states = [hb_ref[q * n_pairs + pr]]
            gens.append(_rwkv_pair_steps(r[:, sl], k[:, sl], v[:, sl], lw[rs, sl], a[rs, sl], g[rs, sl], prm,
                                         valid, states, seq_log2))
    for idx, (out, new_states) in enumerate(_run_interleaved(gens)):
        q, pr = divmod(idx, n_pairs)
        sl = slice(pr * LANE, (pr + 1) * LANE)
        if stacked:
            o_ref[:, sl] = out
            for s, st in enumerate(new_states):
                hb_out_ref[s, 0] = st
        else:
            o_ref[q, :, sl] = out
            hb_ref[idx] = new_states[0]

    if not stacked:
        @pl.when(c == n_c - 1)
        def _():
            for q in range(n_long):
                hb_out_ref[q] = hb_ref[q * n_pairs:(q + 1) * n_pairs]


def rwkv(proj, shift_rows, hb0, params, mu2, wd, wa, wg, seq_len, first_valid):
    n, _ = proj.shape
    n_pr = H_B // 2
    stacked = seq_len < CHUNK
    if stacked:
        assert CHUNK % seq_len == 0 and n % CHUNK == 0 and seq_len & (seq_len - 1) == 0
        seq_log2 = seq_len.bit_length() - 1
        n_pairs, per_step = 1, CHUNK // seq_len
        grid = (n // CHUNK, n_pr)
        rows = lambda g, pr: g
        pcol = lambda g, pr: pr
        hb_block = (per_step, 1, CHUNK, CHUNK)
        hb_map = lambda g, pr: (g, pr, 0, 0)
        s_rows = CHUNK
        srow = lambda g, pr: (g,)
        scratch = []
    else:
        assert seq_len % CHUNK == 0 and n % seq_len == 0
        seq_log2 = 7
        n_c = seq_len // CHUNK
        n_pairs = n_pr
        n_seqs = n // seq_len
        n_long = next(d for d in (4, 2, 1) if n_seqs % d == 0)
        proj = proj.reshape(n_seqs, seq_len, proj.shape[1])
        grid = (n_seqs // n_long, n_c)
        pcol = lambda bi, c: 0
        hb_block = (n_long, n_pr, CHUNK, CHUNK)
        hb_map = lambda bi, c: (bi, 0, 0, 0)
        scratch = [pltpu.VMEM((n_long * n_pr, CHUNK, CHUNK), F32), pltpu.VMEM((3 * n_long, SUB, W_B), F32),
                   pltpu.VMEM((2 * n_long, SUB, LANE), F32)]
    w = n_pairs * LANE
    q0 = 3 * W_A

    def x_spec(col0, width, per_pair):
        blk0 = col0 // width
        if stacked:
            return pl.BlockSpec((CHUNK, width), lambda i, j: (rows(i, j), blk0 + (pcol(i, j) if per_pair else 0)))
        return pl.BlockSpec((n_long, CHUNK, width), lambda i, j: (i, j, blk0))

    def s_spec(col0, width, per_pair):
        blk0 = col0 // width
        if stacked:
            return pl.BlockSpec((s_rows, width), lambda i, j: srow(i, j) + (blk0 + (pcol(i, j) if per_pair else 0),))
        return pl.BlockSpec((n_long, SUB, width), lambda i, j: (i, 0, blk0))

    cols = [(0, w, True), (W_B, w, True), (2 * W_B, w, True), (3 * W_B, LANE, False), (3 * W_B + LANE, LANE, False)]
    in_specs = ([x_spec(q0 + c0, wd_, pp) for c0, wd_, pp in cols]
                + [s_spec(c0, wd_, pp) for c0, wd_, pp in cols]
                + [pl.BlockSpec(hb_block, hb_map),
                   pl.BlockSpec((16, w), lambda i, j: (0, pcol(i, j))),
                   pl.BlockSpec((SUB, LANE), lambda i, j: (0, 0)),
                   pl.BlockSpec((LANE, w), lambda i, j: (0, pcol(i, j))),
                   pl.BlockSpec((LANE, w), lambda i, j: (0, pcol(i, j))),
                   pl.BlockSpec((LANE, w), lambda i, j: (0, pcol(i, j)))])
    kern = functools.partial(_rwkv_kernel, n_pairs=n_pairs, seq_log2=seq_log2, first_valid=first_valid)
    if stacked:
        o_spec = pl.BlockSpec((CHUNK, w), lambda i, j: (rows(i, j), pcol(i, j)))
        o_shape = jax.ShapeDtypeStruct((n, W_B), F32)
    else:
        o_spec = pl.BlockSpec((n_long, CHUNK, w), lambda i, j: (i, j, 0))
        o_shape = jax.ShapeDtypeStruct((n_seqs, seq_len, W_B), F32)
    out, hb = pl.pallas_call(
        kern,
        grid=grid,
        in_specs=in_specs,
        out_specs=[o_spec, pl.BlockSpec(hb_block, hb_map)],
        out_shape=[o_shape, jax.ShapeDtypeStruct(hb0.shape, F32)],
        scratch_shapes=scratch,
        compiler_params=_cparams(("parallel", "arbitrary")),
        name="rwkv",
    )(proj, proj, proj, proj, proj, shift_rows, shift_rows, shift_rows, shift_rows, shift_rows,
      hb0, params, mu2, wd, wa, wg)
    return out.reshape(n, W_B), hb


def _scan_rows(x, n):
    ridx = lax.broadcasted_iota(jnp.int32, (n, 1), 0)
    s = 1
    while s < n:
        x = x + jnp.where(ridx >= s, pltpu.roll(x, s, axis=0), 0.0)
        s *= 2
    return x


def _scan_lanes(x):
    lidx = lax.broadcasted_iota(jnp.int32, (1, LANE), 1)
    s = 1
    while s < LANE:
        x = x + jnp.where(lidx >= s, pltpu.roll(x, s, axis=1), 0.0)
        s *= 2
    return x


def _mlstm_kernel(q_ref, k_ref, v_ref, og_ref, gc_ref, gt_ref, conv0_ref, c0_ref, n0_ref, m0_ref,
                  cw_ref, cb_ref, gb_row_ref, gb_col_ref, nw_ref,
                  o_ref, c_out_ref, n_out_ref, m_out_ref,
                  cext_ref, m_ref, halo_ref, *, first_valid):
    c = pl.program_id(1)
    n_c = pl.num_programs(1)
    n_seq, lq = q_ref.shape[0], q_ref.shape[1]
    rcol = lax.broadcasted_iota(jnp.int32, (lq, 1), 0)
    rglob = c * lq + rcol
    valid_c = rglob >= first_valid
    lrow = lax.broadcasted_iota(jnp.int32, (1, LANE), 1)
    valid_r = ((c * lq + lrow) >= first_valid) & (lrow < lq)
    causal = lax.broadcasted_iota(jnp.int32, (lq, LANE), 1) <= lax.broadcasted_iota(jnp.int32, (lq, LANE), 0)
    lane_is_f = (lrow >= H_C) & (lrow < 2 * H_C)
    row_is_f = lax.broadcasted_iota(jnp.int32, (SUB, 1), 0) >= H_C
    ones_ext = jnp.ones((lq, LANE), F32)
    cw = cw_ref[...]

    @pl.when(c == 0)
    def _():
        for sq in range(n_seq):
            for h in range(H_C):
                cext_ref[sq * H_C + h, :, :DV_C] = c0_ref[0, sq, h]
                cext_ref[sq * H_C + h, :, DV_C:] = jnp.broadcast_to(n0_ref[sq, h, 0:1, :], (DK_C, LANE)).T
            m_ref[sq * H_C:(sq + 1) * H_C] = m0_ref[sq]
            halo_ref[sq] = conv0_ref[sq]

    def seq_terms(sq):
        x = jnp.where(valid_c, jnp.concatenate([q_ref[sq], k_ref[sq]], axis=1), 0.0)
        xcat = jnp.concatenate([halo_ref[sq], x], axis=0)
        conv = cb_ref[...] + x * cw[CONV_W - 1:CONV_W]
        for dlt in range(1, CONV_W):
            conv = conv + pltpu.roll(xcat, dlt, axis=0)[SUB:SUB + lq] * cw[CONV_W - 1 - dlt:CONV_W - dlt]
        halo_ref[sq] = xcat[lq:lq + SUB]
        qk = conv * jax.nn.sigmoid(conv)

        gcol = gc_ref[sq] + gb_row_ref[...]
        gcol = jnp.where(lane_is_f, -_softplus(-gcol), gcol)
        gcol = jnp.where(valid_c, gcol, jnp.where(lane_is_f, 0.0, -jnp.inf))
        bcum_cols = _scan_rows(jnp.where(lane_is_f, gcol, 0.0), lq)
        grow = gt_ref[sq] + gb_col_ref[...]
        grow = jnp.where(row_is_f, -_softplus(-grow), grow)
        grow = jnp.where(valid_r, grow, jnp.where(row_is_f, 0.0, -jnp.inf))
        bcum_rows = _scan_lanes(jnp.where(row_is_f, grow, 0.0))
        return qk, gcol, bcum_cols, grow, bcum_rows

    def head_steps(sq, h, qk, gcol, bcum_cols, grow, bcum_rows):
        slot = sq * H_C + h
        q = qk[:, h * DK_C:(h + 1) * DK_C]
        k = qk[:, QK_C + h * DK_C:QK_C + (h + 1) * DK_C] * (DK_C ** -0.5)
        v_ext = jnp.concatenate([v_ref[sq, :, h * DV_C:(h + 1) * DV_C], ones_ext], axis=1)
        li_c = gcol[:, h:h + 1]
        bc_c = bcum_cols[:, H_C + h:H_C + h + 1]
        li_r = grow[h:h + 1, :]
        bc_r = bcum_rows[H_C + h:H_C + h + 1, :]
        m_prev = m_ref[slot, 0:1, 0:1]
        cext = cext_ref[slot]

        dmat = jnp.where(causal, bc_c - bc_r + li_r, -jnp.inf)
        inter = bc_c + m_prev
        m_row = jnp.maximum(inter, jnp.max(dmat, axis=1, keepdims=True))
        k_pad = _pad_rows(k, CHUNK)
        s = _mm(q, k_pad, NT)
        q_c = _mm(q, cext)
        yield
        w_intra = jnp.exp(dmat - m_row)
        w_inter = jnp.exp(inter - m_row)
        s = s * w_intra
        yield
        num_ext = _mm(s, _pad_rows(v_ext, CHUNK)) + w_inter * q_c
        den = num_ext[:, DV_C:DV_C + 1]
        hh = num_ext[:, :DV_C] / jnp.maximum(jnp.abs(den), jnp.exp(-m_row))

        m_new = m_row[lq - 1:lq, :]
        bc_last = bc_c[lq - 1:lq, :]
        w_end = jnp.exp(bc_last - bc_c + li_c - m_new)
        dec = jnp.exp(bc_last + m_prev - m_new)
        cext_ref[slot] = dec * cext + _mm(k_pad, _pad_rows(w_end * v_ext, CHUNK), TN)
        m_ref[slot] = jnp.broadcast_to(m_new, (SUB, LANE))
        yield

        mu = jnp.mean(hh, axis=1, keepdims=True)
        hc = hh - mu
        yield
        var = jnp.mean(hc * hc, axis=1, keepdims=True)
        hn = hc * lax.rsqrt(var + RMS_EPS) * nw_ref[:, h * DV_C:(h + 1) * DV_C]
        o_ref[sq, :, h * DV_C:(h + 1) * DV_C] = hn * jax.nn.sigmoid(og_ref[sq, :, h * DV_C:(h + 1) * DV_C])

    terms = [seq_terms(sq) for sq in range(n_seq)]
    _run_interleaved([head_steps(sq, h, *terms[sq]) for sq in range(n_seq) for h in range(H_C)])

    @pl.when(c == n_c - 1)
    def _():
        for sq in range(n_seq):
            for h in range(H_C):
                c_out_ref[sq, h] = cext_ref[sq * H_C + h, :, :DV_C]
                n_out_ref[sq, h] = cext_ref[sq * H_C + h, :, DV_C:].T[0:SUB, :]
            m_out_ref[sq] = m_ref[sq * H_C:(sq + 1) * H_C]


def mlstm(proj, gates_t, conv0, c0_all, layer, n0b, m0b, conv_w, conv_b, gb_row, gb_col, norm_w, first_valid):
    bsz, t, _ = proj.shape
    lq = min(t, CHUNK)
    n_c = t // lq
    assert t % lq == 0 and gates_t.shape[2] == n_c * CHUNK
    qk2 = 2 * QK_C
    nq = next(d for d in (4, 2, 1) if bsz % d == 0)
    kern = functools.partial(_mlstm_kernel, first_valid=first_valid)
    const2 = lambda bi, c: (0, 0)
    return pl.pallas_call(
        kern,
        grid=(bsz // nq, n_c),
        in_specs=[pl.BlockSpec((nq, lq, QK_C), lambda bi, c: (bi, c, 0)),
                  pl.BlockSpec((nq, lq, QK_C), lambda bi, c: (bi, c, 1)),
                  pl.BlockSpec((nq, lq, V_C), lambda bi, c: (bi, c, qk2 // V_C)),
                  pl.BlockSpec((nq, lq, V_C), lambda bi, c: (bi, c, qk2 // V_C + 1)),
                  pl.BlockSpec((nq, lq, LANE), lambda bi, c: (bi, c, (qk2 + 2 * V_C) // LANE)),
                  pl.BlockSpec((nq, SUB, LANE), lambda bi, c: (bi, 0, c)),
                  pl.BlockSpec((nq, SUB, qk2), lambda bi, c: (bi, 0, 0)),
                  pl.BlockSpec((1, nq, H_C, DK_C, DV_C), lambda bi, c: (layer, bi, 0, 0, 0)),
                  pl.BlockSpec((nq, H_C, SUB, DK_C), lambda bi, c: (bi, 0, 0, 0)),
                  pl.BlockSpec((nq, H_C, SUB, LANE), lambda bi, c: (bi, 0, 0, 0)),
                  pl.BlockSpec((SUB, qk2), const2),
                  pl.BlockSpec((1, qk2), const2),
                  pl.BlockSpec((1, LANE), const2),
                  pl.BlockSpec((SUB, LANE), const2),
                  pl.BlockSpec((1, V_C), const2)],
        out_specs=[pl.BlockSpec((nq, lq, V_C), lambda bi, c: (bi, c, 0)),
                   pl.BlockSpec((nq, H_C, DK_C, DV_C), lambda bi, c: (bi, 0, 0, 0)),
                   pl.BlockSpec((nq, H_C, SUB, DK_C), lambda bi, c: (bi, 0, 0, 0)),
                   pl.BlockSpec((nq, H_C, SUB, LANE), lambda bi, c: (bi, 0, 0, 0))],
        out_shape=[jax.ShapeDtypeStruct((bsz, t, V_C), F32),
                   jax.ShapeDtypeStruct((bsz, H_C, DK_C, DV_C), F32),
                   jax.ShapeDtypeStruct((bsz, H_C, SUB, DK_C), F32),
                   jax.ShapeDtypeStruct((bsz, H_C, SUB, LANE), F32)],
        scratch_shapes=[pltpu.VMEM((nq * H_C, DK_C, DV_C + LANE), F32),
                        pltpu.VMEM((nq * H_C, SUB, LANE), F32),
                        pltpu.VMEM((nq, SUB, qk2), F32)],
        compiler_params=_cparams(("parallel", "arbitrary")),
        name="mlstm",
    )(proj, proj, proj, proj, proj, gates_t, conv0, c0_all, n0b, m0b, conv_w, conv_b, gb_row, gb_col, norm_w)


def _pad_to(x, axis, size):
    pad = [(0, 0)] * x.ndim
    pad[axis] = (0, size - x.shape[axis])
    return jnp.pad(x, pad)


def _rwkv_params(i, rwkv_mu, rwkv_w0, rwkv_a0, rwkv_k_k, rwkv_k_a, rwkv_r_k, rwkv_ln_w, rwkv_ln_b,
                 w_decay_up, w_iclr_up, w_gate_up):
    mu = rwkv_mu[i]
    rows = [mu[0:W_B], mu[W_B:2 * W_B], mu[2 * W_B:3 * W_B], rwkv_w0[i], rwkv_a0[i], rwkv_k_k[i],
            rwkv_k_a[i], rwkv_r_k[i], rwkv_ln_w[i], rwkv_ln_b[i]]
    pp = _pad_to(jnp.stack(rows, axis=0), 0, 16)
    mu2 = _pad_to(mu[3 * W_B:].reshape(2, LANE), 0, SUB)
    wd = jnp.concatenate([w_decay_up[i], jnp.zeros((R_ICLR, W_B), F32)], axis=0).astype(BF16)
    wa = jnp.concatenate([jnp.zeros((R_DECAY, W_B), F32), w_iclr_up[i]], axis=0).astype(BF16)
    wg = w_gate_up[i].astype(BF16)
    return pp, mu2, wd, wa, wg


def _pair_states(wkv):
    bsz = wkv.shape[0]
    st = jnp.swapaxes(wkv, -1, -2).reshape(bsz, H_B // 2, 2, DH_B, DH_B)
    z = jnp.zeros_like(st[:, :, 0])
    top = jnp.concatenate([st[:, :, 0], z], axis=-1)
    bot = jnp.concatenate([z, st[:, :, 1]], axis=-1)
    return jnp.concatenate([top, bot], axis=-2)


def _unpair_states(hb):
    bsz = hb.shape[0]
    a = hb[:, :, :DH_B, :DH_B]
    b = hb[:, :, DH_B:, DH_B:]
    st = jnp.stack([a, b], axis=2).reshape(bsz, H_B, DH_B, DH_B)
    return jnp.swapaxes(st, -1, -2)


def kernel(x_prompt, x_sample, cache_k, cache_v, page_table, state_wkv, state_shift, state_mlstm_c, state_mlstm_n, state_mlstm_m, state_conv, meta_tokens, g_mix_pre, g_mix_post, g_ffn_pre, g_ffn_post, w_in_ab, w_out_ab, sb_bias, rwkv_mu, rwkv_w0, rwkv_w_decay_up, rwkv_a0, rwkv_w_iclr_up, rwkv_w_gate_up, rwkv_k_k, rwkv_k_a, rwkv_r_k, rwkv_ln_w, rwkv_ln_b, w_in_c, conv_w, conv_b, mlstm_b_i, mlstm_b_f, mlstm_norm_w, w_out_c, w_ffn_gate, w_ffn_up, w_ffn_down):
    bsz, seq, d = x_prompt.shape
    db, ds, _ = x_sample.shape
    depth = g_mix_pre.shape[0]
    tp = PAD_FRONT + N_META + seq
    assert seq % CHUNK == 0 and ds % SUB == 0 and ds <= CHUNK and seq >= CONV_W and ds >= CONV_W

    meta = jnp.broadcast_to(meta_tokens.astype(F32)[None], (bsz, N_META, d))
    xp = jnp.concatenate([jnp.zeros((bsz, PAD_FRONT, d), F32), meta, x_prompt], axis=1).reshape(bsz * tp, d)
    xs = x_sample.reshape(db * ds, d)

    n_layers_ab, n_pool = cache_k.shape[:2]
    pool_k = jnp.transpose(cache_k, (0, 1, 3, 4, 2)).reshape(n_layers_ab, n_pool, W_A, CHUNK)
    pool_v = jnp.transpose(cache_v, (0, 1, 3, 4, 2)).reshape(n_layers_ab, n_pool, W_A, CHUNK)
    outs = {name: [] for name in ("k_p", "v_p", "wkv_p", "sh_p", "c_p", "n_p", "m_p", "cv_p",
                                  "k_s", "v_s", "wkv_s", "sh_s", "c_s", "n_s", "m_s", "cv_s")}
    row = lambda v: v.reshape(1, -1)
    w_in_ab_b, w_out_ab_b, w_out_c_b = w_in_ab.astype(BF16), w_out_ab.astype(BF16), w_out_c.astype(BF16)
    w_gate_b, w_up_b, w_down_b = w_ffn_gate.astype(BF16), w_ffn_up.astype(BF16), w_ffn_down.astype(BF16)

    for layer in range(depth):
        i = layer // 2
        if layer % 2 == 0:
            pp, mu2, wd, wa, wg = _rwkv_params(i, rwkv_mu, rwkv_w0, rwkv_a0, rwkv_k_k, rwkv_k_a, rwkv_r_k,
                                               rwkv_ln_w, rwkv_ln_b, rwkv_w_decay_up, rwkv_w_iclr_up, rwkv_w_gate_up)
            new_x = []
            for grp, x in (("p", xp), ("s", xs)):
                nb, t = (bsz, tp) if grp == "p" else (db, ds)
                proj = norm_matmul(x, row(g_mix_pre[layer]), w_in_ab_b, i).reshape(nb, t, AB_COLS)
                if grp == "p":
                    o_a = sb_prompt(proj, sb_bias[i], PAD_FRONT)
                    shift_rows = jnp.zeros((nb, SUB, B_COLS), F32)
                    hb0 = jnp.zeros((nb, H_B // 2, CHUNK, CHUNK), F32)
                    first_valid = PAD_FRONT
                else:
                    o_a = sb_sample(proj, pool_k, pool_v, i, page_table, sb_bias[i])
                    shift_rows = _pad_to(state_shift[i][:, None, :], 1, t).reshape(nb * t, B_COLS)
                    hb0 = _pair_states(state_wkv[i])
                    first_valid = 0
                o_b, hb = rwkv(proj.reshape(nb * t, AB_COLS), shift_rows, hb0, pp, mu2, wd, wa, wg, t, first_valid)
                new_x.append(proj_res([o_a.reshape(nb * t, W_A), o_b.reshape(nb * t, W_B)],
                                      w_out_ab_b, i, row(g_mix_post[layer]), x))
                outs["k_" + grp].append(proj[:, first_valid:, W_A:2 * W_A].reshape(nb, t - first_valid, H_A, DH_A))
                outs["v_" + grp].append(proj[:, first_valid:, 2 * W_A:3 * W_A].reshape(nb, t - first_valid, H_A, DH_A))
                outs["wkv_" + grp].append(_unpair_states(hb))
                outs["sh_" + grp].append(proj[:, t - 1, 3 * W_A:])
            xp, xs = new_x
        else:
            wc = w_in_c[i]
            ncol = wc.shape[1]
            wperm = jnp.concatenate([wc[:, :2 * QK_C + V_C], wc[:, ncol - V_C:],
                                     wc[:, 2 * QK_C + V_C:ncol - V_C]], axis=1)
            wperm = _pad_to(wperm, 1, C_COLS_PAD).astype(BF16)[None]
            cw = _pad_to(conv_w[i], 0, SUB)
            gb = jnp.concatenate([mlstm_b_i[i], mlstm_b_f[i]])
            gb_row = _pad_to(gb, 0, LANE).reshape(1, LANE)
            gb_col = jnp.broadcast_to(gb[:, None], (SUB, LANE))
            new_x = []
            for grp, x in (("p", xp), ("s", xs)):
                nb, t = (bsz, tp) if grp == "p" else (db, ds)
                proj = norm_matmul(x, row(g_mix_pre[layer]), wperm, 0).reshape(nb, t, C_COLS_PAD)
                g0 = 2 * QK_C + 2 * V_C
                gates_t = jnp.swapaxes(proj[:, :, g0:g0 + SUB], 1, 2)
                if grp == "p":
                    first_valid = PAD_FRONT
                    conv0 = jnp.zeros((nb, SUB, 2 * QK_C), F32)
                    c0_all, c0_layer = jnp.zeros((1, nb, H_C, DK_C, DV_C), F32), 0
                    n0 = jnp.zeros((nb, H_C, DK_C), F32)
                    m0 = jnp.zeros((nb, H_C), F32)
                else:
                    first_valid = 0
                    gates_t = _pad_to(gates_t, 2, CHUNK)
                    conv0 = jnp.concatenate([jnp.zeros((nb, SUB - (CONV_W - 1), 2 * QK_C), F32), state_conv[i]], axis=1)
                    c0_all, c0_layer = state_mlstm_c, i
                    n0, m0 = state_mlstm_n[i], state_mlstm_m[i]
                n0b = jnp.broadcast_to(n0[:, :, None, :], (nb, H_C, SUB, DK_C))
                m0b = jnp.broadcast_to(m0[..., None, None], (nb, H_C, SUB, LANE))
                hmix, c_new, n_new, m_new = mlstm(proj, gates_t, conv0, c0_all, c0_layer, n0b, m0b, cw,
                                                  row(conv_b[i]), gb_row, gb_col, row(mlstm_norm_w[i]), first_valid)
                new_x.append(proj_res([hmix.reshape(nb * t, V_C)], w_out_c_b, i, row(g_mix_post[layer]), x))
                outs["c_" + grp].append(c_new)
                outs["n_" + grp].append(n_new[:, :, 0, :])
                outs["m_" + grp].append(m_new[..., 0, 0])
                outs["cv_" + grp].append(proj[:, t - (CONV_W - 1):, :2 * QK_C])
            xp, xs = new_x
        xp = ffn(xp, row(g_ffn_pre[layer]), w_gate_b, w_up_b, w_down_b, layer, row(g_ffn_post[layer]))
        xs = ffn(xs, row(g_ffn_pre[layer]), w_gate_b, w_up_b, w_down_b, layer, row(g_ffn_post[layer]))

    y_prompt = xp.reshape(bsz, tp, d)[:, PAD_FRONT + N_META:]
    y_sample = xs.reshape(db, ds, d)
    st = lambda name: jnp.stack(outs[name])
    return (y_prompt, y_sample, st("k_p"), st("v_p"), st("wkv_p"), st("sh_p"), st("c_p"), st("n_p"),
            st("m_p"), st("cv_p"), st("k_s"), st("v_s"), st("wkv_s"), st("sh_s"), st("c_s"), st("n_s"),
            st("m_s"), st("cv_s"))
```

```python
import functools

import jax
import jax.numpy as jnp
from jax import lax
from jax.experimental import pallas as pl
from jax.experimental.pallas import tpu as pltpu

F32 = jnp.float32
BF16 = jnp.bfloat16

D_MODEL = 1024
N_META = 16
H_A, DH_A = 8, 64
W_A = H_A * DH_A
H_B, DH_B = 8, 64
W_B = H_B * DH_B
R_DECAY, R_ICLR, R_GATE = 64, 64, 128
B_COLS = 3 * W_B + R_DECAY + R_ICLR + R_GATE
AB_COLS = 3 * W_A + B_COLS
H_C, DK_C, DV_C = 4, 128, 256
QK_C = H_C * DK_C
V_C = H_C * DV_C
CONV_W = 4
C_COLS_PAD = 2 * QK_C + 2 * V_C + 128
RMS_EPS = 1e-6
GN_EPS = 64e-5
LOG2E = 1.4426950408889634

LANE = 128
SUB = 8
CHUNK = 128
PAD_FRONT = CHUNK - N_META
VMEM_LIMIT = 56 * 1024 * 1024

NN = (((1,), (0,)), ((), ()))
NT = (((1,), (1,)), ((), ()))
TN = (((0,), (0,)), ((), ()))


def _cparams(sem):
    return pltpu.CompilerParams(dimension_semantics=sem, vmem_limit_bytes=VMEM_LIMIT)


def _dg(a, b, dims=NN):
    return lax.dot_general(a, b, dims, preferred_element_type=F32)


def _split(x):
    hi = x.astype(BF16)
    lo = (x - hi.astype(F32)).astype(BF16)
    return hi, lo


def _mm(a, b, dims=NN):
    return _dg(a.astype(BF16), b.astype(BF16), dims)


def _mm3(a, b, dims=NN):
    ah, al = _split(a)
    bh, bl = _split(b)
    return _dg(ah, bh, dims) + (_dg(ah, bl, dims) + _dg(al, bh, dims))


def _mm2l(a, b_bf16, dims=NN):
    ah, al = _split(a)
    return _dg(ah, b_bf16, dims) + _dg(al, b_bf16, dims)


_mm_tok = _mm
_mm_state = _mm


def _rms(x, g):
    ms = jnp.mean(x * x, axis=-1, keepdims=True)
    return x * lax.rsqrt(ms + RMS_EPS) * g


def _softplus(z):
    return jnp.maximum(z, 0.0) + jnp.log1p(jnp.exp(-jnp.abs(z)))


def _pick_tm(n, cap=512):
    for tm in (512, 384, 256, 128, 64, 32, 16, 8):
        if tm <= cap and n % tm == 0:
            return tm
    raise ValueError(f"row count {n} not a multiple of 8")


def _pad_rows(x, rows):
    if x.shape[0] == rows:
        return x
    return jnp.concatenate([x, jnp.zeros((rows - x.shape[0],) + x.shape[1:], x.dtype)], axis=0)


def _norm_matmul_kernel(x_ref, g_ref, w_ref, o_ref):
    h = _rms(x_ref[...], g_ref[...]).astype(BF16)
    o_ref[...] = _dg(h, w_ref[0])


def norm_matmul(x, g, w_all, layer):
    n, d = x.shape
    c = w_all.shape[2]
    tm = _pick_tm(n)
    return pl.pallas_call(
        _norm_matmul_kernel,
        grid=(n // tm,),
        in_specs=[pl.BlockSpec((tm, d), lambda i: (i, 0)),
                  pl.BlockSpec((1, d), lambda i: (0, 0)),
                  pl.BlockSpec((1, d, c), lambda i: (layer, 0, 0))],
        out_specs=pl.BlockSpec((tm, c), lambda i: (i, 0)),
        out_shape=jax.ShapeDtypeStruct((n, c), F32),
        compiler_params=_cparams(("parallel",)),
        name="norm_matmul",
    )(x, g, w_all)


def _mix_ffn_kernel(*refs, n_in, n_split):
    ins, ws = refs[:n_in], refs[n_in:2 * n_in]
    gmix_ref, x_ref, gpre_ref, wg_ref, wu_ref, wd_ref, gpost_ref, o_ref = refs[2 * n_in:]
    acc = _dg(ins[0][...].astype(BF16), ws[0][0])
    for a, w in zip(ins[1:], ws[1:]):
        acc = acc + _dg(a[...].astype(BF16), w[0])
    x = x_ref[...] + _rms(acc, gmix_ref[...])
    h = _rms(x, gpre_ref[...]).astype(BF16)
    d_ff = wg_ref.shape[2]
    step = d_ff // n_split
    ff = None
    for s in range(n_split):
        gate = _dg(h, wg_ref[0, :, s * step:(s + 1) * step])
        up = _dg(h, wu_ref[0, :, s * step:(s + 1) * step])
        a = (gate * jax.nn.sigmoid(gate) * up).astype(BF16)
        part = _dg(a, wd_ref[0, s * step:(s + 1) * step, :])
        ff = part if ff is None else ff + part
    o_ref[...] = x + _rms(ff, gpost_ref[...])


def mix_ffn(ins, w_out_all, mix_layer, gmix, x, gpre, wg, wu, wd, layer, gpost):
    n, d = x.shape
    d_ff = wg.shape[2]
    tm = _pick_tm(n, 256)
    n_split = 2 if d_ff % (2 * LANE) == 0 else 1
    n_in = len(ins)
    kw = ins[0].shape[1]
    assert all(a.shape[1] == kw for a in ins) and w_out_all.shape[1] == n_in * kw
    const = lambda i: (0, 0)
    at_layer = lambda i: (layer, 0, 0)
    in_specs = ([pl.BlockSpec((tm, kw), lambda i: (i, 0)) for _ in ins]
                + [pl.BlockSpec((1, kw, d), lambda i, j=j: (mix_layer, j, 0)) for j in range(n_in)]
                + [pl.BlockSpec((1, d), const),
                   pl.BlockSpec((tm, d), lambda i: (i, 0)),
                   pl.BlockSpec((1, d), const),
                   pl.BlockSpec((1, d, d_ff), at_layer),
                   pl.BlockSpec((1, d, d_ff), at_layer),
                   pl.BlockSpec((1, d_ff, d), at_layer),
                   pl.BlockSpec((1, d), const)])
    return pl.pallas_call(
        functools.partial(_mix_ffn_kernel, n_in=n_in, n_split=n_split),
        grid=(n // tm,),
        in_specs=in_specs,
        out_specs=pl.BlockSpec((tm, d), lambda i: (i, 0)),
        out_shape=jax.ShapeDtypeStruct((n, d), F32),
        compiler_params=_cparams(("parallel",)),
        name="mix_ffn",
    )(*ins, *([w_out_all] * n_in), gmix, x, gpre, wg, wu, wd, gpost)


def _later_strict(n_heads):
    n = n_heads * CHUNK
    j = lax.broadcasted_iota(jnp.int32, (n, n), 0)
    s = lax.broadcasted_iota(jnp.int32, (n, n), 1)
    return jnp.where(((j >> 7) == (s >> 7)) & (j > s), 1.0, 0.0).astype(BF16)


def _sb_terms2(z2, mask):
    neg_abs = lax.bitcast_convert_type(lax.bitcast_convert_type(z2, jnp.uint32) | jnp.uint32(0x80000000), F32)
    sp = jnp.maximum(z2, 0.0) + jnp.log2(1.0 + jnp.exp2(neg_abs))
    lb = z2 - sp
    if mask is not None:
        sp = jnp.where(mask, sp, 0.0)
    return lb, sp


def _sb_prompt_kernel(bias_ref, q_ref, k_ref, v_ref, o_ref, acc_ref, ra_ref, rb_ref, *, first_valid, scale):
    hp = pl.program_id(1)
    qi = pl.program_id(2)
    tq = q_ref.shape[1]
    q = (q_ref[0] * (scale * LOG2E)).astype(BF16)
    head_a = lax.broadcasted_iota(jnp.int32, (1, LANE), 1) < DH_A
    head_a2 = lax.broadcasted_iota(jnp.int32, (1, 2 * CHUNK), 1) < CHUNK
    bias2 = jnp.where(head_a2, bias_ref[2 * hp], bias_ref[2 * hp + 1]) * LOG2E
    later = _later_strict(2)
    acc_ref[...] = jnp.zeros_like(acc_ref)
    ra_ref[...] = jnp.zeros_like(ra_ref)
    rb_ref[...] = jnp.zeros_like(rb_ref)

    n_grp = tq // CHUNK

    def process_group(j0, masked):
        kjs = [j0 + g for g in reversed(range(n_grp))]
        starts = [pl.multiple_of(kj * CHUNK, CHUNK) for kj in kjs]
        masks = [None] * n_grp
        if masked:
            t_pos = qi * tq + lax.broadcasted_iota(jnp.int32, (tq, 2 * CHUNK), 0)
            s_loc = lax.broadcasted_iota(jnp.int32, (tq, 2 * CHUNK), 1) & (CHUNK - 1)
            masks = [kj * CHUNK + s_loc < t_pos for kj in kjs]
        z2s = []
        for start in starts:
            k = k_ref[0, pl.ds(start, CHUNK), :]
            k2 = jnp.concatenate([jnp.where(head_a, k, 0.0), jnp.where(head_a, 0.0, k)], axis=0).astype(BF16)
            z2s.append(_dg(q, k2, NT))
        tiles = []
        for z2, mask in zip(z2s, masks):
            lb, sp = _sb_terms2(z2 + bias2, mask)
            within = _dg(sp.astype(BF16), later)
            rs_a = jnp.sum(sp[:, :CHUNK], axis=1, keepdims=True)
            rs_b = jnp.sum(sp[:, CHUNK:], axis=1, keepdims=True)
            tiles.append((lb - within, rs_a, rs_b))
        ra, rb, acc = ra_ref[...], rb_ref[...], acc_ref[...]
        for (ex, rs_a, rs_b), mask, kj, start in zip(tiles, masks, kjs, starts):
            att = jnp.exp2(jnp.concatenate([ex[:, :CHUNK] - ra, ex[:, CHUNK:] - rb], axis=1))
            if mask is not None:
                att = jnp.where(mask, att, 0.0)
            krow = kj * CHUNK + lax.broadcasted_iota(jnp.int32, (CHUNK, 1), 0)
            v = jnp.where(krow >= first_valid, v_ref[0, pl.ds(start, CHUNK), :], 0.0)
            v2 = jnp.concatenate([jnp.where(head_a, v, 0.0), jnp.where(head_a, 0.0, v)], axis=0).astype(BF16)
            acc = acc + _dg(att.astype(BF16), v2)
            ra = ra + rs_a
            rb = rb + rs_b
        ra_ref[...], rb_ref[...], acc_ref[...] = ra, rb, acc

    process_group(qi * n_grp, True)

    def body(i, carry):
        process_group((qi - 1 - i) * n_grp, False)
        return carry

    lax.fori_loop(0, qi, body, 0)
    o_ref[0] = acc_ref[...]


def sb_prompt(proj, bias, first_valid):
    b, t, _ = proj.shape
    tq = 3 * CHUNK if t % (3 * CHUNK) == 0 else CHUNK
    n_hp = H_A // 2
    kern = functools.partial(_sb_prompt_kernel, first_valid=first_valid, scale=DH_A ** -0.5)
    return pl.pallas_call(
        kern,
        grid=(b, n_hp, t // tq),
        in_specs=[pl.BlockSpec(memory_space=pltpu.SMEM),
                  pl.BlockSpec((1, tq, LANE), lambda bi, h, qi: (bi, qi, h)),
                  pl.BlockSpec((1, t, LANE), lambda bi, h, qi: (bi, 0, n_hp + h)),
                  pl.BlockSpec((1, t, LANE), lambda bi, h, qi: (bi, 0, 2 * n_hp + h))],
        out_specs=pl.BlockSpec((1, tq, LANE), lambda bi, h, qi: (bi, qi, h)),
        out_shape=jax.ShapeDtypeStruct((b, t, W_A), F32),
        scratch_shapes=[pltpu.VMEM((tq, LANE), F32)] * 3,
        compiler_params=_cparams(("parallel", "parallel", "arbitrary")),
        name="sb_prompt",
    )(bias, proj, proj, proj)


def _sb_sample_kernel(pt_ref, q_ref, kn_ref, vn_ref, bias_ref, *rest, scale, n_g):
    kp_refs, vp_refs = rest[:n_g], rest[n_g:2 * n_g]
    o_ref, qbd_ref, acc_ref, r_ref = rest[2 * n_g:]
    p = pl.program_id(1)
    n_steps = pl.num_programs(1)
    tq = q_ref.shape[1]
    rows = H_A * tq
    later = _later_strict(1)
    col_head = lax.broadcasted_iota(jnp.int32, (1, W_A), 1) >> 6
    bias2 = bias_ref[...] * LOG2E

    def scores(z2s, mask):
        terms = [_sb_terms2(z2 + bias2, mask) for z2 in z2s]
        withins = [_dg(sp.astype(BF16), later) for _, sp in terms]
        return [(lb - within, jnp.sum(sp, axis=1, keepdims=True)) for (lb, sp), within in zip(terms, withins)]

    @pl.when(p == 0)
    def _():
        q = q_ref[0] * (scale * LOG2E)
        qbd_ref[...] = jnp.concatenate(
            [jnp.where(col_head == h, q, 0.0) for h in range(H_A)], axis=0).astype(BF16)
        kn = _pad_rows(kn_ref[0], CHUNK).astype(BF16)
        vn = _pad_rows(vn_ref[0], CHUNK).astype(BF16)
        t_pos = lax.broadcasted_iota(jnp.int32, (rows, CHUNK), 0) & (tq - 1)
        s_pos = lax.broadcasted_iota(jnp.int32, (rows, CHUNK), 1)
        mask = s_pos < t_pos
        (ex, rs), = scores([_dg(qbd_ref[...], kn, NT)], mask)
        att = jnp.where(mask, jnp.exp2(ex), 0.0)
        acc_ref[...] = _dg(att.astype(BF16), vn)
        r_ref[...] = jnp.broadcast_to(rs, r_ref.shape)

    qbd = qbd_ref[...]
    parts = scores([_dg(qbd, kp_refs[j][0, 0].astype(BF16)) for j in range(n_g)], None)
    r = r_ref[...]
    atts = []
    for ex, rs in parts:
        atts.append(jnp.exp2(ex - r).astype(BF16))
        r = r + rs
    acc = acc_ref[...]
    for j in range(n_g):
        acc = acc + _dg(atts[j], vp_refs[j][0, 0].astype(BF16), NT)
    acc_ref[...] = acc
    r_ref[...] = r

    @pl.when(p == n_steps - 1)
    def _():
        out = jnp.where(col_head == 0, acc[0:tq], 0.0)
        for h in range(1, H_A):
            out = out + jnp.where(col_head == h, acc[h * tq:(h + 1) * tq], 0.0)
        o_ref[0] = out


def sb_sample(proj, pool_k, pool_v, layer, page_table, bias):
    db, tq, _ = proj.shape
    n_pages = page_table.shape[1]
    page = pool_k.shape[3]
    assert page == CHUNK and tq % SUB == 0 and tq <= CHUNK and tq & (tq - 1) == 0 and DH_A == 64 and DH_B == 64
    n_g = next(g for g in (16, 8, 4, 2, 1) if n_pages % g == 0)
    rows = H_A * tq
    bias_rows = jnp.broadcast_to(jnp.repeat(bias, tq)[:, None], (rows, CHUNK)).astype(F32)
    pt = page_table.reshape(-1).astype(jnp.int32)

    def page_spec(j):
        return pl.BlockSpec((1, 1, W_A, page), lambda bi, p, pt_ref: (
            layer, pt_ref[bi * n_pages + (n_pages - 1 - (p * n_g + j))], 0, 0))

    grid_spec = pltpu.PrefetchScalarGridSpec(
        num_scalar_prefetch=1,
        grid=(db, n_pages // n_g),
        in_specs=([pl.BlockSpec((1, tq, W_A), lambda bi, p, pt_ref: (bi, 0, 0)),
                   pl.BlockSpec((1, tq, W_A), lambda bi, p, pt_ref: (bi, 0, 1)),
                   pl.BlockSpec((1, tq, W_A), lambda bi, p, pt_ref: (bi, 0, 2)),
                   pl.BlockSpec((rows, CHUNK), lambda bi, p, pt_ref: (0, 0))]
                  + [page_spec(j) for j in range(n_g)] * 2),
        out_specs=pl.BlockSpec((1, tq, W_A), lambda bi, p, pt_ref: (bi, 0, 0)),
        scratch_shapes=[pltpu.VMEM((rows, W_A), BF16),
                        pltpu.VMEM((rows, W_A), F32),
                        pltpu.VMEM((rows, CHUNK), F32)],
    )
    return pl.pallas_call(
        functools.partial(_sb_sample_kernel, scale=DH_A ** -0.5, n_g=n_g),
        grid_spec=grid_spec,
        out_shape=jax.ShapeDtypeStruct((db, tq, W_A), F32),
        compiler_params=_cparams(("parallel", "arbitrary")),
        name="sb_sample",
    )(pt, proj, proj, proj, bias_rows, *([pool_k] * n_g), *([pool_v] * n_g))


def _run_interleaved(gens):
    results = [None] * len(gens)
    active = list(range(len(gens)))
    while active:
        for i in list(active):
            try:
                next(gens[i])
            except StopIteration as stop:
                results[i] = stop.value
                active.remove(i)
    return results


def _unit_lower_inverse_steps(n_mats, row, col, seq_log2):
    eye = jnp.where(row == col, 1.0, 0.0)
    log_base = min(4, seq_log2)
    ds = [jnp.where((row >> log_base) == (col >> log_base), n, 0.0) for n in n_mats]
    xs = [eye - d for d in ds]
    for _ in range(log_base - 1):
        ds = [_mm(d, d) for d in ds]
        yield
        xs = [x + _mm(x, d) for x, d in zip(xs, ds)]
        yield
    for lg in range(log_base, seq_log2):
        off = (row >> (lg + 1)) == (col >> (lg + 1))
        off = jnp.where(off, (row >> lg) & 1, 0) > ((col >> lg) & 1)
        ts = [_mm(jnp.where(off, n, 0.0), x) for n, x in zip(n_mats, xs)]
        yield
        xs = [x - _mm(x, t) for x, t in zip(xs, ts)]
        yield
    return xs


def _rwkv_pair_steps(r, k, v, lw, a, g, prm, valid, states, seq_log2):
    k_k, k_a, r_k, ln_w, ln_b = prm
    seq_len = 1 << seq_log2
    n_seq = CHUNK >> seq_log2
    row = lax.broadcasted_iota(jnp.int32, (CHUNK, CHUNK), 0)
    col = lax.broadcasted_iota(jnp.int32, (CHUNK, CHUNK), 1)
    head_a = lax.broadcasted_iota(jnp.int32, (1, LANE), 1) < DH_B
    same_head = (row >> 6) == (col >> 6)
    same_seq = (row >> seq_log2) == (col >> seq_log2)
    seg_sum = jnp.where(same_head, 1.0, 0.0).astype(BF16)
    seg_avg = jnp.where(same_head, 1.0 / DH_B, 0.0).astype(BF16)

    kk = k * k_k
    kk = kk / jnp.maximum(jnp.sqrt(_mm2l(kk * kk, seg_sum)), 1e-12)
    k_mod = k * (1.0 + (a - 1.0) * k_a)
    coef = _mm2l(r * k_mod * r_k, seg_sum)
    k_mod = jnp.where(valid, k_mod, 0.0)
    b = jnp.where(valid, kk * a, 0.0)

    incl = jnp.where(same_seq & (col <= row), 1.0, 0.0).astype(BF16)
    lw_hi = lw.astype(BF16)
    lw_mid = lw - lw_hi.astype(F32)
    lw_mid_b = lw_mid.astype(BF16)
    lw_lo = (lw_mid - lw_mid_b.astype(F32)).astype(BF16)
    cum = _dg(incl, lw_hi) + (_dg(incl, lw_mid_b) + _dg(incl, lw_lo))
    yield
    ends =[cum[(s + 1) * seq_len - 1:(s + 1) * seq_len, :] for s in range(n_seq)]
    cum_end = ends[0] if n_seq == 1 else jnp.concatenate(
        [jnp.broadcast_to(e, (seq_len, LANE)) for e in ends], axis=0)
    a_t = kk * jnp.exp(cum - lw)
    r_t = r * jnp.exp(cum)
    inv_p = jnp.exp(-cum)
    b_t = b * inv_p
    k_t = k_mod * inv_p
    to_end = jnp.exp(cum_end - cum)
    k_end = k_mod * to_end
    b_end = b * to_end
    cum_t = cum.T

    strict = same_seq & (col < row)
    incl_m = same_seq & (col <= row)
    lhs2 = jnp.concatenate([a_t, r_t], axis=0)
    n_mats, a_aks, a_rbs, a_rks = [], [], [], []
    rhs2 = jnp.concatenate([b_t, k_t], axis=0)
    for head_mask in (head_a, jnp.logical_not(head_a)):
        m = _mm_tok(jnp.where(head_mask, lhs2, 0.0), rhs2, NT)
        m_b, m_k = m[:, :CHUNK], m[:, CHUNK:]
        n_mats.append(jnp.where(strict, m_b[:CHUNK], 0.0))
        a_aks.append(jnp.where(strict, m_k[:CHUNK], 0.0))
        a_rbs.append(jnp.where(incl_m, m_b[CHUNK:], 0.0))
        a_rks.append(jnp.where(incl_m, m_k[CHUNK:], 0.0))
    yield
    t_invs = yield from _unit_lower_inverse_steps(n_mats, row, col, seq_log2)
    parts = list(zip(t_invs, a_aks, a_rbs, a_rks))

    def sel(xa, xb):
        return jnp.where(head_a, xa, xb)

    if n_seq == 1:
        st = _mm_state(lhs2, states[0])
        a_h, r_h = st[:CHUNK], st[CHUNK:]
    else:
        a_hs, r_hs = [], []
        for s in range(n_seq):
            sl = slice(s * seq_len, (s + 1) * seq_len)
            st = _mm_state(jnp.concatenate([a_t[sl], r_t[sl]], axis=0), states[s])
            a_hs.append(st[:seq_len])
            r_hs.append(st[seq_len:])
        a_h = jnp.concatenate(a_hs, axis=0)
        r_h = jnp.concatenate(r_hs, axis=0)

    rhs = a_h + sel(_mm_tok(parts[0][1], v), _mm_tok(parts[1][1], v))
    yield
    u = sel(_mm_tok(parts[0][0], rhs), _mm_tok(parts[1][0], rhs))
    yield
    vu = jnp.concatenate([v, u], axis=0)
    y = r_h + sel(
        _mm_tok(jnp.concatenate([parts[0][3], -parts[0][2]], axis=1), vu),
        _mm_tok(jnp.concatenate([parts[1][3], -parts[1][2]], axis=1), vu))

    new_states = []
    for s in range(n_seq):
        sl = slice(s * seq_len, (s + 1) * seq_len)
        upd = _mm_state(jnp.concatenate([k_end[sl], -b_end[sl]], axis=0),
                        jnp.concatenate([v[sl], u[sl]], axis=0), TN)
        p_end_col = jnp.exp(cum_t[:, (s + 1) * seq_len - 1:(s + 1) * seq_len])
        new_states.append(states[s] * p_end_col + jnp.where(same_head, upd, 0.0))
    yield

    mean = _mm2l(y, seg_avg)
    yc = y - mean
    yield
    var = _mm2l(yc * yc, seg_avg)
    yn = yc * lax.rsqrt(var + GN_EPS) * ln_w + ln_b
    return (yn + coef * v) * g, new_states


def _rwkv_kernel(r_ref, k_ref, v_ref, da_ref, g_ref, sr_ref, sk_ref, sv_ref, sda_ref, sg_ref,
                 hb0_ref, p_ref, mu2_ref, wd_ref, wa_ref, wg_ref, o_ref, hb_out_ref, *scratch,
                 n_pairs, seq_log2, first_valid):
    stacked = seq_log2 < 7
    n_long = 1 if stacked else r_ref.shape[0]
    rcol = lax.broadcasted_iota(jnp.int32, (CHUNK, 1), 0)
    if stacked:
        valid = rcol >= 0
        seq_start = (rcol & ((1 << seq_log2) - 1)) == 0
    else:
        hb_ref, carry_ref, carry2_ref = scratch
        c = pl.program_id(1)
        n_c = pl.num_programs(1)
        rglob = c * CHUNK + rcol
        valid = rglob >= first_valid

        @pl.when(c == 0)
        def _():
            for q in range(n_long):
                hb_ref[q * n_pairs:(q + 1) * n_pairs] = hb0_ref[q]
                for j, s_ref in enumerate((sr_ref, sk_ref, sv_ref)):
                    carry_ref[3 * q + j] = jnp.broadcast_to(s_ref[q, 0:1, :], carry_ref.shape[1:])
                for j, s_ref in enumerate((sda_ref, sg_ref)):
                    carry2_ref[2 * q + j] = jnp.broadcast_to(s_ref[q, 0:1, :], carry2_ref.shape[1:])

    P = p_ref[...]
    mu2 = mu2_ref[...]
    mus = (P[0:1], P[1:2], P[2:3], mu2[0:1], mu2[1:2])
    x_refs = (r_ref, k_ref, v_ref, da_ref, g_ref)
    s_refs = (sr_ref, sk_ref, sv_ref, sda_ref, sg_ref)
    us_all = []
    for q in range(n_long):
        us = []
        for j in range(5):
            x = x_refs[j][...] if stacked else x_refs[j][q]
            rolled = pltpu.roll(x, 1, axis=0)
            if stacked:
                prev = jnp.where(seq_start, s_refs[j][...], rolled)
            else:
                cref, jj = (carry_ref, 3 * q + j) if j < 3 else (carry2_ref, 2 * q + j - 3)
                prev = jnp.where(rcol == 0, cref[jj][SUB - 1:SUB, :], rolled)
                prev = jnp.where(rglob == first_valid, s_refs[j][q, 0:1, :], prev)
                cref[jj] = x[CHUNK - SUB:CHUNK, :]
            us.append(x + (prev - x) * mus[j])
        us_all.append(us)

    cat = lambda xs: xs[0] if len(xs) == 1 else jnp.concatenate(xs, axis=0)
    u_da = cat([us[3] for us in us_all])
    u_g = cat([us[4] for us in us_all])
    valid_all = cat([valid] * n_long)
    xw = P[3:4] + _mm(jnp.tanh(u_da), wd_ref[...])
    w_log = -_softplus(-xw) - 0.5
    lw = jnp.where(valid_all, -jnp.exp(w_log), 0.0)
    a = jax.nn.sigmoid(P[4:5] + _mm(u_da, wa_ref[...]))
    g = _mm(jax.nn.sigmoid(u_g), wg_ref[...])

    gens = []
    for q in range(n_long):
        rs = slice(q * CHUNK, (q + 1) * CHUNK)
        r, k, v = us_all[q][:3]
        for pr in range(n_pairs):
            sl = slice(pr * LANE, (pr + 1) * LANE)
            prm = tuple(P[i:i + 1, sl] for i in range(5, 10))
            if stacked:
                states = [hb0_ref[s, 0] for s in range(CHUNK >> seq_log2)]
            else:
                states = [hb_ref[q * n_pairs + pr]]
            gens.append(_rwkv_pair_steps(r[:, sl], k[:, sl], v[:, sl], lw[rs, sl], a[rs, sl], g[rs, sl], prm,
                                         valid, states, seq_log2))
    for idx, (out, new_states) in enumerate(_run_interleaved(gens)):
        q, pr = divmod(idx, n_pairs)
        sl = slice(pr * LANE, (pr + 1) * LANE)
        if stacked:
            o_ref[:, sl] = out
            for s, st in enumerate(new_states):
                hb_out_ref[s, 0] = st
        else:
            o_ref[q, :, sl] = out
            hb_ref[idx] = new_states[0]

    if not stacked:
        @pl.when(c == n_c - 1)
        def _():
            for q in range(n_long):
                hb_out_ref[q] = hb_ref[q * n_pairs:(q + 1) * n_pairs]


def rwkv(proj, shift_rows, hb0, params, mu2, wd, wa, wg, seq_len, first_valid):
    n, _ = proj.shape
    n_pr = H_B // 2
    stacked = seq_len < CHUNK
    if stacked:
        assert CHUNK % seq_len == 0 and n % CHUNK == 0 and seq_len & (seq_len - 1) == 0
        seq_log2 = seq_len.bit_length() - 1
        n_pairs, per_step = 1, CHUNK // seq_len
        grid = (n // CHUNK, n_pr)
        rows = lambda g, pr: g
        pcol = lambda g, pr: pr
        hb_block = (per_step, 1, CHUNK, CHUNK)
        hb_map = lambda g, pr: (g, pr, 0, 0)
        s_rows = CHUNK
        srow = lambda g, pr: (g,)
        scratch = []
    else:
        assert seq_len % CHUNK == 0 and n % seq_len == 0
        seq_log2 = 7
        n_c = seq_len // CHUNK
        n_pairs = n_pr
        n_seqs = n // seq_len
        n_long = next(d for d in (4, 2, 1) if n_seqs % d == 0)
        proj = proj.reshape(n_seqs, seq_len, proj.shape[1])
        grid = (n_seqs // n_long, n_c)
        pcol = lambda bi, c: 0
        hb_block = (n_long, n_pr, CHUNK, CHUNK)
        hb_map = lambda bi, c: (bi, 0, 0, 0)
        scratch = [pltpu.VMEM((n_long * n_pr, CHUNK, CHUNK), F32), pltpu.VMEM((3 * n_long, SUB, W_B), F32),
                   pltpu.VMEM((2 * n_long, SUB, LANE), F32)]
    w = n_pairs * LANE
    q0 = 3 * W_A

    def x_spec(col0, width, per_pair):
        blk0 = col0 // width
        if stacked:
            return pl.BlockSpec((CHUNK, width), lambda i, j: (rows(i, j), blk0 + (pcol(i, j) if per_pair else 0)))
        return pl.BlockSpec((n_long, CHUNK, width), lambda i, j: (i, j, blk0))

    def s_spec(col0, width, per_pair):
        blk0 = col0 // width
        if stacked:
            return pl.BlockSpec((s_rows, width), lambda i, j: srow(i, j) + (blk0 + (pcol(i, j) if per_pair else 0),))
        return pl.BlockSpec((n_long, SUB, width), lambda i, j: (i, 0, blk0))

    cols = [(0, w, True), (W_B, w, True), (2 * W_B, w, True), (3 * W_B, LANE, False), (3 * W_B + LANE, LANE, False)]
    in_specs = ([x_spec(q0 + c0, wd_, pp) for c0, wd_, pp in cols]
                + [s_spec(c0, wd_, pp) for c0, wd_, pp in cols]
                + [pl.BlockSpec(hb_block, hb_map),
                   pl.BlockSpec((16, w), lambda i, j: (0, pcol(i, j))),
                   pl.BlockSpec((SUB, LANE), lambda i, j: (0, 0)),
                   pl.BlockSpec((LANE, w), lambda i, j: (0, pcol(i, j))),
                   pl.BlockSpec((LANE, w), lambda i, j: (0, pcol(i, j))),
                   pl.BlockSpec((LANE, w), lambda i, j: (0, pcol(i, j)))])
    kern = functools.partial(_rwkv_kernel, n_pairs=n_pairs, seq_log2=seq_log2, first_valid=first_valid)
    if stacked:
        o_spec = pl.BlockSpec((CHUNK, w), lambda i, j: (rows(i, j), pcol(i, j)))
        o_shape = jax.ShapeDtypeStruct((n, W_B), F32)
    else:
        o_spec = pl.BlockSpec((n_long, CHUNK, w), lambda i, j: (i, j, 0))
        o_shape = jax.ShapeDtypeStruct((n_seqs, seq_len, W_B), F32)
    out, hb = pl.pallas_call(
        kern,
        grid=grid,
        in_specs=in_specs,
        out_specs=[o_spec, pl.BlockSpec(hb_block, hb_map)],
        out_shape=[o_shape, jax.ShapeDtypeStruct(hb0.shape, F32)],
        scratch_shapes=scratch,
        compiler_params=_cparams(("parallel", "arbitrary")),
        name="rwkv",
    )(proj, proj, proj, proj, proj, shift_rows, shift_rows, shift_rows, shift_rows, shift_rows,
      hb0, params, mu2, wd, wa, wg)
    return out.reshape(n, W_B), hb


def _scan_rows(x, n):
    ridx = lax.broadcasted_iota(jnp.int32, (n, 1), 0)
    s = 1
    while s < n:
        x = x + jnp.where(ridx >= s, pltpu.roll(x, s, axis=0), 0.0)
        s *= 2
    return x


def _scan_lanes(x):
    lidx = lax.broadcasted_iota(jnp.int32, (1, LANE), 1)
    s = 1
    while s < LANE:
        x = x + jnp.where(lidx >= s, pltpu.roll(x, s, axis=1), 0.0)
        s *= 2
    return x


def _mlstm_kernel(q_ref, k_ref, v_ref, og_ref, gc_ref, gt_ref, conv0_ref, c0_ref, n0_ref, m0_ref,
                  cw_ref, cb_ref, gb_row_ref, gb_col_ref, nw_ref,
                  o_ref, c_out_ref, n_out_ref, m_out_ref,
                  cext_ref, m_ref, halo_ref, *, first_valid):
    c = pl.program_id(1)
    n_c = pl.num_programs(1)
    n_seq, lq = q_ref.shape[0], q_ref.shape[1]
    rcol = lax.broadcasted_iota(jnp.int32, (lq, 1), 0)
    rglob = c * lq + rcol
    valid_c = rglob >= first_valid
    lrow = lax.broadcasted_iota(jnp.int32, (1, LANE), 1)
    valid_r = ((c * lq + lrow) >= first_valid) & (lrow < lq)
    causal = lax.broadcasted_iota(jnp.int32, (lq, LANE), 1) <= lax.broadcasted_iota(jnp.int32, (lq, LANE), 0)
    lane_is_f = (lrow >= H_C) & (lrow < 2 * H_C)
    row_is_f = lax.broadcasted_iota(jnp.int32, (SUB, 1), 0) >= H_C
    ones_ext = jnp.ones((lq, LANE), F32)
    cw = cw_ref[...]

    @pl.when(c == 0)
    def _():
        for sq in range(n_seq):
            for h in range(H_C):
                cext_ref[sq * H_C + h, :, :DV_C] = c0_ref[0, sq, h]
                cext_ref[sq * H_C + h, :, DV_C:] = jnp.broadcast_to(n0_ref[sq, h, 0:1, :], (DK_C, LANE)).T
            m_ref[sq * H_C:(sq + 1) * H_C] = m0_ref[sq]
            halo_ref[sq] = conv0_ref[sq]

    def seq_terms(sq):
        x = jnp.where(valid_c, jnp.concatenate([q_ref[sq], k_ref[sq]], axis=1), 0.0)
        xcat = jnp.concatenate([halo_ref[sq], x], axis=0)
        conv = cb_ref[...] + x * cw[CONV_W - 1:CONV_W]
        for dlt in range(1, CONV_W):
            conv = conv + pltpu.roll(xcat, dlt, axis=0)[SUB:SUB + lq] * cw[CONV_W - 1 - dlt:CONV_W - dlt]
        halo_ref[sq] = xcat[lq:lq + SUB]
        qk = conv * jax.nn.sigmoid(conv)

        gcol = gc_ref[sq] + gb_row_ref[...]
        gcol = jnp.where(lane_is_f, -_softplus(-gcol), gcol)
        gcol = jnp.where(valid_c, gcol, jnp.where(lane_is_f, 0.0, -jnp.inf))
        bcum_cols = _scan_rows(jnp.where(lane_is_f, gcol, 0.0), lq)
        grow = gt_ref[sq] + gb_col_ref[...]
        grow = jnp.where(row_is_f, -_softplus(-grow), grow)
        grow = jnp.where(valid_r, grow, jnp.where(row_is_f, 0.0, -jnp.inf))
        bcum_rows = _scan_lanes(jnp.where(row_is_f, grow, 0.0))
        return qk, gcol, bcum_cols, grow, bcum_rows

    def head_steps(sq, h, qk, gcol, bcum_cols, grow, bcum_rows):
        slot = sq * H_C + h
        q = qk[:, h * DK_C:(h + 1) * DK_C]
        k = qk[:, QK_C + h * DK_C:QK_C + (h + 1) * DK_C] * (DK_C ** -0.5)
        v_ext = jnp.concatenate([v_ref[sq, :, h * DV_C:(h + 1) * DV_C], ones_ext], axis=1)
        li_c = gcol[:, h:h + 1]
        bc_c = bcum_cols[:, H_C + h:H_C + h + 1]
        li_r = grow[h:h + 1, :]
        bc_r = bcum_rows[H_C + h:H_C + h + 1, :]
        m_prev = m_ref[slot, 0:1, 0:1]
        cext = cext_ref[slot]

        dmat = jnp.where(causal, bc_c - bc_r + li_r, -jnp.inf)
        inter = bc_c + m_prev
        m_row = jnp.maximum(inter, jnp.max(dmat, axis=1, keepdims=True))
        k_pad = _pad_rows(k, CHUNK)
        s = _mm(q, k_pad, NT)
        q_c = _mm(q, cext)
        yield
        w_intra = jnp.exp(dmat - m_row)
        w_inter = jnp.exp(inter - m_row)
        s = s * w_intra
        yield
        num_ext = _mm(s, _pad_rows(v_ext, CHUNK)) + w_inter * q_c
        den = num_ext[:, DV_C:DV_C + 1]
        hh = num_ext[:, :DV_C] / jnp.maximum(jnp.abs(den), jnp.exp(-m_row))

        m_new = m_row[lq - 1:lq, :]
        bc_last = bc_c[lq - 1:lq, :]
        w_end = jnp.exp(bc_last - bc_c + li_c - m_new)
        dec = jnp.exp(bc_last + m_prev - m_new)
        cext_ref[slot] = dec * cext + _mm(k_pad, _pad_rows(w_end * v_ext, CHUNK), TN)
        m_ref[slot] = jnp.broadcast_to(m_new, (SUB, LANE))
        yield

        mu = jnp.mean(hh, axis=1, keepdims=True)
        hc = hh - mu
        yield
        var = jnp.mean(hc * hc, axis=1, keepdims=True)
        hn = hc * lax.rsqrt(var + RMS_EPS) * nw_ref[:, h * DV_C:(h + 1) * DV_C]
        o_ref[sq, :, h * DV_C:(h + 1) * DV_C] = hn * jax.nn.sigmoid(og_ref[sq, :, h * DV_C:(h + 1) * DV_C])

    terms = [seq_terms(sq) for sq in range(n_seq)]
    _run_interleaved([head_steps(sq, h, *terms[sq]) for sq in range(n_seq) for h in range(H_C)])

    @pl.when(c == n_c - 1)
    def _():
        for sq in range(n_seq):
            for h in range(H_C):
                c_out_ref[sq, h] = cext_ref[sq * H_C + h, :, :DV_C]
                n_out_ref[sq, h] = cext_ref[sq * H_C + h, :, DV_C:].T[0:SUB, :]
            m_out_ref[sq] = m_ref[sq * H_C:(sq + 1) * H_C]


def mlstm(proj, gates_t, conv0, c0_all, layer, n0b, m0b, conv_w, conv_b, gb_row, gb_col, norm_w, first_valid):
    bsz, t, _ = proj.shape
    lq = min(t, CHUNK)
    n_c = t // lq
    assert t % lq == 0 and gates_t.shape[2] == n_c * CHUNK
    qk2 = 2 * QK_C
    nq = next(d for d in (4, 2, 1) if bsz % d == 0)
    kern = functools.partial(_mlstm_kernel, first_valid=first_valid)
    const2 = lambda bi, c: (0, 0)
    return pl.pallas_call(
        kern,
        grid=(bsz // nq, n_c),
        in_specs=[pl.BlockSpec((nq, lq, QK_C), lambda bi, c: (bi, c, 0)),
                  pl.BlockSpec((nq, lq, QK_C), lambda bi, c: (bi, c, 1)),
                  pl.BlockSpec((nq, lq, V_C), lambda bi, c: (bi, c, qk2 // V_C)),
                  pl.BlockSpec((nq, lq, V_C), lambda bi, c: (bi, c, qk2 // V_C + 1)),
                  pl.BlockSpec((nq, lq, LANE), lambda bi, c: (bi, c, (qk2 + 2 * V_C) // LANE)),
                  pl.BlockSpec((nq, SUB, LANE), lambda bi, c: (bi, 0, c)),
                  pl.BlockSpec((nq, SUB, qk2), lambda bi, c: (bi, 0, 0)),
                  pl.BlockSpec((1, nq, H_C, DK_C, DV_C), lambda bi, c: (layer, bi, 0, 0, 0)),
                  pl.BlockSpec((nq, H_C, SUB, DK_C), lambda bi, c: (bi, 0, 0, 0)),
                  pl.BlockSpec((nq, H_C, SUB, LANE), lambda bi, c: (bi, 0, 0, 0)),
                  pl.BlockSpec((SUB, qk2), const2),
                  pl.BlockSpec((1, qk2), const2),
                  pl.BlockSpec((1, LANE), const2),
                  pl.BlockSpec((SUB, LANE), const2),
                  pl.BlockSpec((1, V_C), const2)],
        out_specs=[pl.BlockSpec((nq, lq, V_C), lambda bi, c: (bi, c, 0)),
                   pl.BlockSpec((nq, H_C, DK_C, DV_C), lambda bi, c: (bi, 0, 0, 0)),
                   pl.BlockSpec((nq, H_C, SUB, DK_C), lambda bi, c: (bi, 0, 0, 0)),
                   pl.BlockSpec((nq, H_C, SUB, LANE), lambda bi, c: (bi, 0, 0, 0))],
        out_shape=[jax.ShapeDtypeStruct((bsz, t, V_C), F32),
                   jax.ShapeDtypeStruct((bsz, H_C, DK_C, DV_C), F32),
                   jax.ShapeDtypeStruct((bsz, H_C, SUB, DK_C), F32),
                   jax.ShapeDtypeStruct((bsz, H_C, SUB, LANE), F32)],
        scratch_shapes=[pltpu.VMEM((nq * H_C, DK_C, DV_C + LANE), F32),
                        pltpu.VMEM((nq * H_C, SUB, LANE), F32),
                        pltpu.VMEM((nq, SUB, qk2), F32)],
        compiler_params=_cparams(("parallel", "arbitrary")),
        name="mlstm",
    )(proj, proj, proj, proj, proj, gates_t, conv0, c0_all, n0b, m0b, conv_w, conv_b, gb_row, gb_col, norm_w)


def _pad_to(x, axis, size):
    pad = [(0, 0)] * x.ndim
    pad[axis] = (0, size - x.shape[axis])
    return jnp.pad(x, pad)


def _rwkv_params(i, rwkv_mu, rwkv_w0, rwkv_a0, rwkv_k_k, rwkv_k_a, rwkv_r_k, rwkv_ln_w, rwkv_ln_b,
                 w_decay_up, w_iclr_up, w_gate_up):
    mu = rwkv_mu[i]
    rows = [mu[0:W_B], mu[W_B:2 * W_B], mu[2 * W_B:3 * W_B], rwkv_w0[i], rwkv_a0[i], rwkv_k_k[i],
            rwkv_k_a[i], rwkv_r_k[i], rwkv_ln_w[i], rwkv_ln_b[i]]
    pp = _pad_to(jnp.stack(rows, axis=0), 0, 16)
    mu2 = _pad_to(mu[3 * W_B:].reshape(2, LANE), 0, SUB)
    wd = jnp.concatenate([w_decay_up[i], jnp.zeros((R_ICLR, W_B), F32)], axis=0).astype(BF16)
    wa = jnp.concatenate([jnp.zeros((R_DECAY, W_B), F32), w_iclr_up[i]], axis=0).astype(BF16)
    wg = w_gate_up[i].astype(BF16)
    return pp, mu2, wd, wa, wg


def _pair_states(wkv):
    bsz = wkv.shape[0]
    st = jnp.swapaxes(wkv, -1, -2).reshape(bsz, H_B // 2, 2, DH_B, DH_B)
    z = jnp.zeros_like(st[:, :, 0])
    top = jnp.concatenate([st[:, :, 0], z], axis=-1)
    bot = jnp.concatenate([z, st[:, :, 1]], axis=-1)
    return jnp.concatenate([top, bot], axis=-2)


def _unpair_states(hb):
    bsz = hb.shape[0]
    a = hb[:, :, :DH_B, :DH_B]
    b = hb[:, :, DH_B:, DH_B:]
    st = jnp.stack([a, b], axis=2).reshape(bsz, H_B, DH_B, DH_B)
    return jnp.swapaxes(st, -1, -2)


def kernel(x_prompt, x_sample, cache_k, cache_v, page_table, state_wkv, state_shift, state_mlstm_c, state_mlstm_n, state_mlstm_m, state_conv, meta_tokens, g_mix_pre, g_mix_post, g_ffn_pre, g_ffn_post, w_in_ab, w_out_ab, sb_bias, rwkv_mu, rwkv_w0, rwkv_w_decay_up, rwkv_a0, rwkv_w_iclr_up, rwkv_w_gate_up, rwkv_k_k, rwkv_k_a, rwkv_r_k, rwkv_ln_w, rwkv_ln_b, w_in_c, conv_w, conv_b, mlstm_b_i, mlstm_b_f, mlstm_norm_w, w_out_c, w_ffn_gate, w_ffn_up, w_ffn_down):
    bsz, seq, d = x_prompt.shape
    db, ds, _ = x_sample.shape
    depth = g_mix_pre.shape[0]
    tp = PAD_FRONT + N_META + seq
    assert seq % CHUNK == 0 and ds % SUB == 0 and ds <= CHUNK and seq >= CONV_W and ds >= CONV_W

    meta = jnp.broadcast_to(meta_tokens.astype(F32)[None], (bsz, N_META, d))
    xp = jnp.concatenate([jnp.zeros((bsz, PAD_FRONT, d), F32), meta, x_prompt], axis=1).reshape(bsz * tp, d)
    xs = x_sample.reshape(db * ds, d)

    n_layers_ab, n_pool = cache_k.shape[:2]
    pool_k = jnp.transpose(cache_k, (0, 1, 3, 4, 2)).reshape(n_layers_ab, n_pool, W_A, CHUNK)
    pool_v = jnp.transpose(cache_v, (0, 1, 3, 4, 2)).reshape(n_layers_ab, n_pool, W_A, CHUNK)
    outs = {name: [] for name in ("k_p", "v_p", "wkv_p", "sh_p", "c_p", "n_p", "m_p", "cv_p",
                                  "k_s", "v_s", "wkv_s", "sh_s", "c_s", "n_s", "m_s", "cv_s")}
    row = lambda v: v.reshape(1, -1)
    w_in_ab_b, w_out_ab_b, w_out_c_b = w_in_ab.astype(BF16), w_out_ab.astype(BF16), w_out_c.astype(BF16)
    w_gate_b, w_up_b, w_down_b = w_ffn_gate.astype(BF16), w_ffn_up.astype(BF16), w_ffn_down.astype(BF16)

    for layer in range(depth):
        i = layer // 2
        ffn_args = (row(g_ffn_pre[layer]), w_gate_b, w_up_b, w_down_b, layer, row(g_ffn_post[layer]))
        if layer % 2 == 0:
            pp, mu2, wd, wa, wg = _rwkv_params(i, rwkv_mu, rwkv_w0, rwkv_a0, rwkv_k_k, rwkv_k_a, rwkv_r_k,
                                               rwkv_ln_w, rwkv_ln_b, rwkv_w_decay_up, rwkv_w_iclr_up, rwkv_w_gate_up)
            new_x = []
            for grp, x in (("p", xp), ("s", xs)):
                nb, t = (bsz, tp) if grp == "p" else (db, ds)
                proj = norm_matmul(x, row(g_mix_pre[layer]), w_in_ab_b, i).reshape(nb, t, AB_COLS)
                if grp == "p":
                    o_a = sb_prompt(proj, sb_bias[i], PAD_FRONT)
                    shift_rows = jnp.zeros((nb, SUB, B_COLS), F32)
                    hb0 = jnp.zeros((nb, H_B // 2, CHUNK, CHUNK), F32)
                    first_valid = PAD_FRONT
                else:
                    o_a = sb_sample(proj, pool_k, pool_v, i, page_table, sb_bias[i])
                    shift_rows = _pad_to(state_shift[i][:, None, :], 1, t).reshape(nb * t, B_COLS)
                    hb0 = _pair_states(state_wkv[i])
                    first_valid = 0
                o_b, hb = rwkv(proj.reshape(nb * t, AB_COLS), shift_rows, hb0, pp, mu2, wd, wa, wg, t, first_valid)
                new_x.append(mix_ffn([o_a.reshape(nb * t, W_A), o_b.reshape(nb * t, W_B)], w_out_ab_b, i,
                                     row(g_mix_post[layer]), x, *ffn_args))
                outs["k_" + grp].append(proj[:, first_valid:, W_A:2 * W_A].reshape(nb, t - first_valid, H_A, DH_A))
                outs["v_" + grp].append(proj[:, first_valid:, 2 * W_A:3 * W_A].reshape(nb, t - first_valid, H_A, DH_A))
                outs["wkv_" + grp].append(_unpair_states(hb))
                outs["sh_" + grp].append(proj[:, t - 1, 3 * W_A:])
            xp, xs = new_x
        else:
            wc = w_in_c[i]
            ncol = wc.shape[1]
            wperm = jnp.concatenate([wc[:, :2 * QK_C + V_C], wc[:, ncol - V_C:],
                                     wc[:, 2 * QK_C + V_C:ncol - V_C]], axis=1)
            wperm = _pad_to(wperm, 1, C_COLS_PAD).astype(BF16)[None]
            cw = _pad_to(conv_w[i], 0, SUB)
            gb = jnp.concatenate([mlstm_b_i[i], mlstm_b_f[i]])
            gb_row = _pad_to(gb, 0, LANE).reshape(1, LANE)
            gb_col = jnp.broadcast_to(gb[:, None], (SUB, LANE))
            new_x = []
            for grp, x in (("p", xp), ("s", xs)):
                nb, t = (bsz, tp) if grp == "p" else (db, ds)
                proj = norm_matmul(x, row(g_mix_pre[layer]), wperm, 0).reshape(nb, t, C_COLS_PAD)
                g0 = 2 * QK_C + 2 * V_C
                gates_t = jnp.swapaxes(proj[:, :, g0:g0 + SUB], 1, 2)
                if grp == "p":
                    first_valid = PAD_FRONT
                    conv0 = jnp.zeros((nb, SUB, 2 * QK_C), F32)
                    c0_all, c0_layer = jnp.zeros((1, nb, H_C, DK_C, DV_C), F32), 0
                    n0 = jnp.zeros((nb, H_C, DK_C), F32)
                    m0 = jnp.zeros((nb, H_C), F32)
                else:
                    first_valid = 0
                    gates_t = _pad_to(gates_t, 2, CHUNK)
                    conv0 = jnp.concatenate([jnp.zeros((nb, SUB - (CONV_W - 1), 2 * QK_C), F32), state_conv[i]], axis=1)
                    c0_all, c0_layer = state_mlstm_c, i
                    n0, m0 = state_mlstm_n[i], state_mlstm_m[i]
                n0b = jnp.broadcast_to(n0[:, :, None, :], (nb, H_C, SUB, DK_C))
                m0b = jnp.broadcast_to(m0[..., None, None], (nb, H_C, SUB, LANE))
                hmix, c_new, n_new, m_new = mlstm(proj, gates_t, conv0, c0_all, c0_layer, n0b, m0b, cw,
                                                  row(conv_b[i]), gb_row, gb_col, row(mlstm_norm_w[i]), first_valid)
                new_x.append(mix_ffn([hmix.reshape(nb * t, V_C)], w_out_c_b, i, row(g_mix_post[layer]), x,
                                     *ffn_args))
                outs["c_" + grp].append(c_new)
                outs["n_" + grp].append(n_new[:, :, 0, :])
                outs["m_" + grp].append(m_new[..., 0, 0])
                outs["cv_" + grp].append(proj[:, t - (CONV_W - 1):, :2 * QK_C])
            xp, xs = new_x

    y_prompt = xp.reshape(bsz, tp, d)[:, PAD_FRONT + N_META:]
    y_sample = xs.reshape(db, ds, d)
    st = lambda name: jnp.stack(outs[name])
    return (y_prompt, y_sample, st("k_p"), st("v_p"), st("wkv_p"), st("sh_p"), st("c_p"), st("n_p"),
            st("m_p"), st("cv_p"), st("k_s"), st("v_s"), st("wkv_s"), st("sh_s"), st("c_s"), st("n_s"),
            st("m_s"), st("cv_s"))
```

```python
import functools

import jax
import jax.numpy as jnp
from jax import lax
from jax.experimental import pallas as pl
from jax.experimental.pallas import tpu as pltpu

F32 = jnp.float32
BF16 = jnp.bfloat16

D_MODEL = 1024
N_META = 16
H_A, DH_A = 8, 64
W_A = H_A * DH_A
H_B, DH_B = 8, 64
W_B = H_B * DH_B
R_DECAY, R_ICLR, R_GATE = 64, 64, 128
B_COLS = 3 * W_B + R_DECAY + R_ICLR + R_GATE
AB_COLS = 3 * W_A + B_COLS
H_C, DK_C, DV_C = 4, 128, 256
QK_C = H_C * DK_C
V_C = H_C * DV_C
CONV_W = 4
C_COLS_PAD = 2 * QK_C + 2 * V_C + 128
RMS_EPS = 1e-6
GN_EPS = 64e-5
LOG2E = 1.4426950408889634

LANE = 128
SUB = 8
CHUNK = 128
PAD_FRONT = CHUNK - N_META
VMEM_LIMIT = 56 * 1024 * 1024

NN = (((1,), (0,)), ((), ()))
NT = (((1,), (1,)), ((), ()))
TN = (((0,), (0,)), ((), ()))


def _cparams(sem):
    return pltpu.CompilerParams(dimension_semantics=sem, vmem_limit_bytes=VMEM_LIMIT)


def _dg(a, b, dims=NN):
    return lax.dot_general(a, b, dims, preferred_element_type=F32)


def _split(x):
    hi = x.astype(BF16)
    lo = (x - hi.astype(F32)).astype(BF16)
    return hi, lo


def _mm(a, b, dims=NN):
    return _dg(a.astype(BF16), b.astype(BF16), dims)


def _mm3(a, b, dims=NN):
    ah, al = _split(a)
    bh, bl = _split(b)
    return _dg(ah, bh, dims) + (_dg(ah, bl, dims) + _dg(al, bh, dims))


def _mm2l(a, b_bf16, dims=NN):
    ah, al = _split(a)
    return _dg(ah, b_bf16, dims) + _dg(al, b_bf16, dims)


_mm_tok = _mm
_mm_state = _mm


def _rms(x, g):
    ms = jnp.mean(x * x, axis=-1, keepdims=True)
    return x * lax.rsqrt(ms + RMS_EPS) * g


def _softplus(z):
    return jnp.maximum(z, 0.0) + jnp.log1p(jnp.exp(-jnp.abs(z)))


def _pick_tm(n, cap=512):
    for tm in (512, 384, 256, 128, 64, 32, 16, 8):
        if tm <= cap and n % tm == 0:
            return tm
    raise ValueError(f"row count {n} not a multiple of 8")


def _pad_rows(x, rows):
    if x.shape[0] == rows:
        return x
    return jnp.concatenate([x, jnp.zeros((rows - x.shape[0],) + x.shape[1:], x.dtype)], axis=0)


def _norm_matmul_kernel(x_ref, g_ref, w_ref, o_ref):
    h = _rms(x_ref[...], g_ref[...]).astype(BF16)
    o_ref[...] = _dg(h, w_ref[0])


def norm_matmul(x, g, w_all, layer):
    n, d = x.shape
    c = w_all.shape[2]
    tm = _pick_tm(n)
    return pl.pallas_call(
        _norm_matmul_kernel,
        grid=(n // tm,),
        in_specs=[pl.BlockSpec((tm, d), lambda i: (i, 0)),
                  pl.BlockSpec((1, d), lambda i: (0, 0)),
                  pl.BlockSpec((1, d, c), lambda i: (layer, 0, 0))],
        out_specs=pl.BlockSpec((tm, c), lambda i: (i, 0)),
        out_shape=jax.ShapeDtypeStruct((n, c), F32),
        compiler_params=_cparams(("parallel",)),
        name="norm_matmul",
    )(x, g, w_all)


def _mix_ffn_kernel(*refs, n_in, n_split):
    ins, ws = refs[:n_in], refs[n_in:2 * n_in]
    gmix_ref, x_ref, gpre_ref, wg_ref, wu_ref, wd_ref, gpost_ref, o_ref = refs[2 * n_in:]
    acc = _dg(ins[0][...].astype(BF16), ws[0][0])
    for a, w in zip(ins[1:], ws[1:]):
        acc = acc + _dg(a[...].astype(BF16), w[0])
    x = x_ref[...] + _rms(acc, gmix_ref[...])
    h = _rms(x, gpre_ref[...]).astype(BF16)
    d_ff = wg_ref.shape[2]
    step = d_ff // n_split
    ff = None
    for s in range(n_split):
        gate = _dg(h, wg_ref[0, :, s * step:(s + 1) * step])
        up = _dg(h, wu_ref[0, :, s * step:(s + 1) * step])
        a = (gate * jax.nn.sigmoid(gate) * up).astype(BF16)
        part = _dg(a, wd_ref[0, s * step:(s + 1) * step, :])
        ff = part if ff is None else ff + part
    o_ref[...] = x + _rms(ff, gpost_ref[...])


def mix_ffn(ins, w_out_all, mix_layer, gmix, x, gpre, wg, wu, wd, layer, gpost):
    n, d = x.shape
    d_ff = wg.shape[2]
    tm = _pick_tm(n, 256)
    n_split = 2 if d_ff % (2 * LANE) == 0 else 1
    n_in = len(ins)
    kw = ins[0].shape[1]
    assert all(a.shape[1] == kw for a in ins) and w_out_all.shape[1] == n_in * kw
    const = lambda i: (0, 0)
    at_layer = lambda i: (layer, 0, 0)
    in_specs = ([pl.BlockSpec((tm, kw), lambda i: (i, 0)) for _ in ins]
                + [pl.BlockSpec((1, kw, d), lambda i, j=j: (mix_layer, j, 0)) for j in range(n_in)]
                + [pl.BlockSpec((1, d), const),
                   pl.BlockSpec((tm, d), lambda i: (i, 0)),
                   pl.BlockSpec((1, d), const),
                   pl.BlockSpec((1, d, d_ff), at_layer),
                   pl.BlockSpec((1, d, d_ff), at_layer),
                   pl.BlockSpec((1, d_ff, d), at_layer),
                   pl.BlockSpec((1, d), const)])
    return pl.pallas_call(
        functools.partial(_mix_ffn_kernel, n_in=n_in, n_split=n_split),
        grid=(n // tm,),
        in_specs=in_specs,
        out_specs=pl.BlockSpec((tm, d), lambda i: (i, 0)),
        out_shape=jax.ShapeDtypeStruct((n, d), F32),
        compiler_params=_cparams(("parallel",)),
        name="mix_ffn",
    )(*ins, *([w_out_all] * n_in), gmix, x, gpre, wg, wu, wd, gpost)


def _later_strict(n_heads):
    n = n_heads * CHUNK
    j = lax.broadcasted_iota(jnp.int32, (n, n), 0)
    s = lax.broadcasted_iota(jnp.int32, (n, n), 1)
    return jnp.where(((j >> 7) == (s >> 7)) & (j > s), 1.0, 0.0).astype(BF16)


def _sb_terms2(z2, mask):
    neg_abs = lax.bitcast_convert_type(lax.bitcast_convert_type(z2, jnp.uint32) | jnp.uint32(0x80000000), F32)
    sp = jnp.maximum(z2, 0.0) + jnp.log2(1.0 + jnp.exp2(neg_abs))
    lb = z2 - sp
    if mask is not None:
        sp = jnp.where(mask, sp, 0.0)
    return lb, sp


def _sb_prompt_kernel(bias_ref, q_ref, k_ref, v_ref, o_ref, acc_ref, ra_ref, rb_ref, *, first_valid, scale):
    hp = pl.program_id(1)
    qi = pl.program_id(2)
    tq = q_ref.shape[1]
    q = (q_ref[0] * (scale * LOG2E)).astype(BF16)
    head_a = lax.broadcasted_iota(jnp.int32, (1, LANE), 1) < DH_A
    head_a2 = lax.broadcasted_iota(jnp.int32, (1, 2 * CHUNK), 1) < CHUNK
    bias2 = jnp.where(head_a2, bias_ref[2 * hp], bias_ref[2 * hp + 1]) * LOG2E
    later = _later_strict(2)
    acc_ref[...] = jnp.zeros_like(acc_ref)
    ra_ref[...] = jnp.zeros_like(ra_ref)
    rb_ref[...] = jnp.zeros_like(rb_ref)

    n_grp = tq // CHUNK

    def process_group(j0, masked):
        kjs = [j0 + g for g in reversed(range(n_grp))]
        starts = [pl.multiple_of(kj * CHUNK, CHUNK) for kj in kjs]
        masks = [None] * n_grp
        if masked:
            t_pos = qi * tq + lax.broadcasted_iota(jnp.int32, (tq, 2 * CHUNK), 0)
            s_loc = lax.broadcasted_iota(jnp.int32, (tq, 2 * CHUNK), 1) & (CHUNK - 1)
            masks = [kj * CHUNK + s_loc < t_pos for kj in kjs]
        z2s = []
        for start in starts:
            k = k_ref[0, pl.ds(start, CHUNK), :]
            k2 = jnp.concatenate([jnp.where(head_a, k, 0.0), jnp.where(head_a, 0.0, k)], axis=0).astype(BF16)
            z2s.append(_dg(q, k2, NT))
        tiles = []
        for z2, mask in zip(z2s, masks):
            lb, sp = _sb_terms2(z2 + bias2, mask)
            within = _dg(sp.astype(BF16), later)
            rs_a = jnp.sum(sp[:, :CHUNK], axis=1, keepdims=True)
            rs_b = jnp.sum(sp[:, CHUNK:], axis=1, keepdims=True)
            tiles.append((lb - within, rs_a, rs_b))
        ra, rb, acc = ra_ref[...], rb_ref[...], acc_ref[...]
        for (ex, rs_a, rs_b), mask, kj, start in zip(tiles, masks, kjs, starts):
            att = jnp.exp2(jnp.concatenate([ex[:, :CHUNK] - ra, ex[:, CHUNK:] - rb], axis=1))
            if mask is not None:
                att = jnp.where(mask, att, 0.0)
            krow = kj * CHUNK + lax.broadcasted_iota(jnp.int32, (CHUNK, 1), 0)
            v = jnp.where(krow >= first_valid, v_ref[0, pl.ds(start, CHUNK), :], 0.0)
            v2 = jnp.concatenate([jnp.where(head_a, v, 0.0), jnp.where(head_a, 0.0, v)], axis=0).astype(BF16)
            acc = acc + _dg(att.astype(BF16), v2)
            ra = ra + rs_a
            rb = rb + rs_b
        ra_ref[...], rb_ref[...], acc_ref[...] = ra, rb, acc

    process_group(qi * n_grp, True)

    def body(i, carry):
        process_group((qi - 1 - i) * n_grp, False)
        return carry

    lax.fori_loop(0, qi, body, 0)
    o_ref[0] = acc_ref[...]


def sb_prompt(proj, bias, first_valid):
    b, t, _ = proj.shape
    tq = 3 * CHUNK if t % (3 * CHUNK) == 0 else CHUNK
    n_hp = H_A // 2
    kern = functools.partial(_sb_prompt_kernel, first_valid=first_valid, scale=DH_A ** -0.5)
    return pl.pallas_call(
        kern,
        grid=(b, n_hp, t // tq),
        in_specs=[pl.BlockSpec(memory_space=pltpu.SMEM),
                  pl.BlockSpec((1, tq, LANE), lambda bi, h, qi: (bi, qi, h)),
                  pl.BlockSpec((1, t, LANE), lambda bi, h, qi: (bi, 0, n_hp + h)),
                  pl.BlockSpec((1, t, LANE), lambda bi, h, qi: (bi, 0, 2 * n_hp + h))],
        out_specs=pl.BlockSpec((1, tq, LANE), lambda bi, h, qi: (bi, qi, h)),
        out_shape=jax.ShapeDtypeStruct((b, t, W_A), F32),
        scratch_shapes=[pltpu.VMEM((tq, LANE), F32)] * 3,
        compiler_params=_cparams(("parallel", "parallel", "arbitrary")),
        name="sb_prompt",
    )(bias, proj, proj, proj)


def _sb_sample_kernel(pt_ref, q_ref, kn_ref, vn_ref, bias_ref, *rest, scale, n_g):
    kp_refs, vp_refs = rest[:n_g], rest[n_g:2 * n_g]
    o_ref, qbd_ref, acc_ref, r_ref = rest[2 * n_g:]
    p = pl.program_id(1)
    n_steps = pl.num_programs(1)
    tq = q_ref.shape[1]
    rows = H_A * tq
    later = _later_strict(1)
    col_head = lax.broadcasted_iota(jnp.int32, (1, W_A), 1) >> 6
    bias2 = bias_ref[...] * LOG2E

    def scores(z2s, mask):
        terms = [_sb_terms2(z2 + bias2, mask) for z2 in z2s]
        withins = [_dg(sp.astype(BF16), later) for _, sp in terms]
        return [(lb - within, jnp.sum(sp, axis=1, keepdims=True)) for (lb, sp), within in zip(terms, withins)]

    @pl.when(p == 0)
    def _():
        q = q_ref[0] * (scale * LOG2E)
        qbd_ref[...] = jnp.concatenate(
            [jnp.where(col_head == h, q, 0.0) for h in range(H_A)], axis=0).astype(BF16)
        kn = _pad_rows(kn_ref[0], CHUNK).astype(BF16)
        vn = _pad_rows(vn_ref[0], CHUNK).astype(BF16)
        t_pos = lax.broadcasted_iota(jnp.int32, (rows, CHUNK), 0) & (tq - 1)
        s_pos = lax.broadcasted_iota(jnp.int32, (rows, CHUNK), 1)
        mask = s_pos < t_pos
        (ex, rs), = scores([_dg(qbd_ref[...], kn, NT)], mask)
        att = jnp.where(mask, jnp.exp2(ex), 0.0)
        acc_ref[...] = _dg(att.astype(BF16), vn)
        r_ref[...] = jnp.broadcast_to(rs, r_ref.shape)

    qbd = qbd_ref[...]
    parts = scores([_dg(qbd, kp_refs[j][0, 0].astype(BF16)) for j in range(n_g)], None)
    r = r_ref[...]
    atts = []
    for ex, rs in parts:
        atts.append(jnp.exp2(ex - r).astype(BF16))
        r = r + rs
    acc = acc_ref[...]
    for j in range(n_g):
        acc = acc + _dg(atts[j], vp_refs[j][0, 0].astype(BF16), NT)
    acc_ref[...] = acc
    r_ref[...] = r

    @pl.when(p == n_steps - 1)
    def _():
        out = jnp.where(col_head == 0, acc[0:tq], 0.0)
        for h in range(1, H_A):
            out = out + jnp.where(col_head == h, acc[h * tq:(h + 1) * tq], 0.0)
        o_ref[0] = out


def sb_sample(proj, pool_k, pool_v, layer, page_table, bias):
    db, tq, _ = proj.shape
    n_pages = page_table.shape[1]
    page = pool_k.shape[3]
    assert page == CHUNK and tq % SUB == 0 and tq <= CHUNK and tq & (tq - 1) == 0 and DH_A == 64 and DH_B == 64
    n_g = next(g for g in (16, 8, 4, 2, 1) if n_pages % g == 0)
    rows = H_A * tq
    bias_rows = jnp.broadcast_to(jnp.repeat(bias, tq)[:, None], (rows, CHUNK)).astype(F32)
    pt = page_table.reshape(-1).astype(jnp.int32)

    def page_spec(j):
        return pl.BlockSpec((1, 1, W_A, page), lambda bi, p, pt_ref: (
            layer, pt_ref[bi * n_pages + (n_pages - 1 - (p * n_g + j))], 0, 0))

    grid_spec = pltpu.PrefetchScalarGridSpec(
        num_scalar_prefetch=1,
        grid=(db, n_pages // n_g),
        in_specs=([pl.BlockSpec((1, tq, W_A), lambda bi, p, pt_ref: (bi, 0, 0)),
                   pl.BlockSpec((1, tq, W_A), lambda bi, p, pt_ref: (bi, 0, 1)),
                   pl.BlockSpec((1, tq, W_A), lambda bi, p, pt_ref: (bi, 0, 2)),
                   pl.BlockSpec((rows, CHUNK), lambda bi, p, pt_ref: (0, 0))]
                  + [page_spec(j) for j in range(n_g)] * 2),
        out_specs=pl.BlockSpec((1, tq, W_A), lambda bi, p, pt_ref: (bi, 0, 0)),
        scratch_shapes=[pltpu.VMEM((rows, W_A), BF16),
                        pltpu.VMEM((rows, W_A), F32),
                        pltpu.VMEM((rows, CHUNK), F32)],
    )
    return pl.pallas_call(
        functools.partial(_sb_sample_kernel, scale=DH_A ** -0.5, n_g=n_g),
        grid_spec=grid_spec,
        out_shape=jax.ShapeDtypeStruct((db, tq, W_A), F32),
        compiler_params=_cparams(("parallel", "arbitrary")),
        name="sb_sample",
    )(pt, proj, proj, proj, bias_rows, *([pool_k] * n_g), *([pool_v] * n_g))


def _run_interleaved(gens):
    results = [None] * len(gens)
    active = list(range(len(gens)))
    while active:
        for i in list(active):
            try:
                next(gens[i])
            except StopIteration as stop:
                results[i] = stop.value
                active.remove(i)
    return results


def _unit_lower_inverse_steps(n_mats, row, col, seq_log2):
    eye = jnp.where(row == col, 1.0, 0.0)
    log_base = min(4, seq_log2)
    ds = [jnp.where((row >> log_base) == (col >> log_base), n, 0.0) for n in n_mats]
    xs = [eye - d for d in ds]
    for _ in range(log_base - 1):
        ds = [_mm(d, d) for d in ds]
        yield
        xs = [x + _mm(x, d) for x, d in zip(xs, ds)]
        yield
    for lg in range(log_base, seq_log2):
        off = (row >> (lg + 1)) == (col >> (lg + 1))
        off = jnp.where(off, (row >> lg) & 1, 0) > ((col >> lg) & 1)
        ts = [_mm(jnp.where(off, n, 0.0), x) for n, x in zip(n_mats, xs)]
        yield
        xs = [x - _mm(x, t) for x, t in zip(xs, ts)]
        yield
    return xs


def _rwkv_pair_steps(r, k, v, lw, a, g, prm, valid, states, seq_log2):
    k_k, k_a, r_k, ln_w, ln_b = prm
    seq_len = 1 << seq_log2
    n_seq = CHUNK >> seq_log2
    row = lax.broadcasted_iota(jnp.int32, (CHUNK, CHUNK), 0)
    col = lax.broadcasted_iota(jnp.int32, (CHUNK, CHUNK), 1)
    head_a = lax.broadcasted_iota(jnp.int32, (1, LANE), 1) < DH_B
    same_head = (row >> 6) == (col >> 6)
    same_seq = (row >> seq_log2) == (col >> seq_log2)
    seg_sum = jnp.where(same_head, 1.0, 0.0).astype(BF16)
    seg_avg = jnp.where(same_head, 1.0 / DH_B, 0.0).astype(BF16)

    kk = k * k_k
    kk = kk / jnp.maximum(jnp.sqrt(_mm2l(kk * kk, seg_sum)), 1e-12)
    k_mod = k * (1.0 + (a - 1.0) * k_a)
    coef = _mm2l(r * k_mod * r_k, seg_sum)
    k_mod = jnp.where(valid, k_mod, 0.0)
    b = jnp.where(valid, kk * a, 0.0)

    incl = jnp.where(same_seq & (col <= row), 1.0, 0.0).astype(BF16)
    lw_hi = lw.astype(BF16)
    lw_mid = lw - lw_hi.astype(F32)
    lw_mid_b = lw_mid.astype(BF16)
    lw_lo = (lw_mid - lw_mid_b.astype(F32)).astype(BF16)
    cum = _dg(incl, lw_hi) + (_dg(incl, lw_mid_b) + _dg(incl, lw_lo))
    yield
    ends =[cum[(s + 1) * seq_len - 1:(s + 1) * seq_len, :] for s in range(n_seq)]
    cum_end = ends[0] if n_seq == 1 else jnp.concatenate(
        [jnp.broadcast_to(e, (seq_len, LANE)) for e in ends], axis=0)
    a_t = kk * jnp.exp(cum - lw)
    r_t = r * jnp.exp(cum)
    mids = [cum[s * seq_len + seq_len // 2 - 1:s * seq_len + seq_len // 2, :] for s in range(n_seq)]
    cum_mid = mids[0] if n_seq == 1 else jnp.concatenate(
        [jnp.broadcast_to(m, (seq_len, LANE)) for m in mids], axis=0)
    a_m = kk * jnp.exp(cum - lw - cum_mid)
    r_m = r * jnp.exp(cum - cum_mid)
    inv_p = jnp.exp(cum_mid - cum)
    b_t = b * inv_p
    k_t = k_mod * inv_p
    to_end = jnp.exp(cum_end - cum)
    k_end = k_mod * to_end
    b_end = b * to_end
    cum_t = cum.T

    strict = same_seq & (col < row)
    incl_m = same_seq & (col <= row)
    lhs2 = jnp.concatenate([a_t, r_t], axis=0)
    lhs2_m = jnp.concatenate([a_m, r_m], axis=0)
    n_mats, a_aks, a_rbs, a_rks = [], [], [], []
    rhs2 = jnp.concatenate([b_t, k_t], axis=0)
    for head_mask in (head_a, jnp.logical_not(head_a)):
        m = _mm_tok(jnp.where(head_mask, lhs2_m, 0.0), rhs2, NT)
        m_b, m_k = m[:, :CHUNK], m[:, CHUNK:]
        n_mats.append(jnp.where(strict, m_b[:CHUNK], 0.0))
        a_aks.append(jnp.where(strict, m_k[:CHUNK], 0.0))
        a_rbs.append(jnp.where(incl_m, m_b[CHUNK:], 0.0))
        a_rks.append(jnp.where(incl_m, m_k[CHUNK:], 0.0))
    yield
    t_invs = yield from _unit_lower_inverse_steps(n_mats, row, col, seq_log2)
    parts = list(zip(t_invs, a_aks, a_rbs, a_rks))

    def sel(xa, xb):
        return jnp.where(head_a, xa, xb)

    if n_seq == 1:
        st = _mm_state(lhs2, states[0])
        a_h, r_h = st[:CHUNK], st[CHUNK:]
    else:
        a_hs, r_hs = [], []
        for s in range(n_seq):
            sl = slice(s * seq_len, (s + 1) * seq_len)
            st = _mm_state(jnp.concatenate([a_t[sl], r_t[sl]], axis=0), states[s])
            a_hs.append(st[:seq_len])
            r_hs.append(st[seq_len:])
        a_h = jnp.concatenate(a_hs, axis=0)
        r_h = jnp.concatenate(r_hs, axis=0)

    rhs = a_h + sel(_mm_tok(parts[0][1], v), _mm_tok(parts[1][1], v))
    yield
    u = sel(_mm_tok(parts[0][0], rhs), _mm_tok(parts[1][0], rhs))
    yield
    vu = jnp.concatenate([v, u], axis=0)
    y = r_h + sel(
        _mm_tok(jnp.concatenate([parts[0][3], -parts[0][2]], axis=1), vu),
        _mm_tok(jnp.concatenate([parts[1][3], -parts[1][2]], axis=1), vu))

    new_states = []
    for s in range(n_seq):
        sl = slice(s * seq_len, (s + 1) * seq_len)
        upd = _mm_state(jnp.concatenate([k_end[sl], -b_end[sl]], axis=0),
                        jnp.concatenate([v[sl], u[sl]], axis=0), TN)
        p_end_col = jnp.exp(cum_t[:, (s + 1) * seq_len - 1:(s + 1) * seq_len])
        new_states.append(states[s] * p_end_col + jnp.where(same_head, upd, 0.0))
    yield

    mean = _mm2l(y, seg_avg)
    yc = y - mean
    yield
    var = _mm2l(yc * yc, seg_avg)
    yn = yc * lax.rsqrt(var + GN_EPS) * ln_w + ln_b
    return (yn + coef * v) * g, new_states


def _rwkv_kernel(r_ref, k_ref, v_ref, da_ref, g_ref, sr_ref, sk_ref, sv_ref, sda_ref, sg_ref,
                 hb0_ref, p_ref, mu2_ref, wd_ref, wa_ref, wg_ref, o_ref, hb_out_ref, *scratch,
                 n_pairs, seq_log2, first_valid):
    stacked = seq_log2 < 7
    n_long = 1 if stacked else r_ref.shape[0]
    rcol = lax.broadcasted_iota(jnp.int32, (CHUNK, 1), 0)
    if stacked:
        valid = rcol >= 0
        seq_start = (rcol & ((1 << seq_log2) - 1)) == 0
    else:
        hb_ref, carry_ref, carry2_ref = scratch
        c = pl.program_id(1)
        n_c = pl.num_programs(1)
        rglob = c * CHUNK + rcol
        valid = rglob >= first_valid

        @pl.when(c == 0)
        def _():
            for q in range(n_long):
                hb_ref[q * n_pairs:(q + 1) * n_pairs] = hb0_ref[q]
                for j, s_ref in enumerate((sr_ref, sk_ref, sv_ref)):
                    carry_ref[3 * q + j] = jnp.broadcast_to(s_ref[q, 0:1, :], carry_ref.shape[1:])
                for j, s_ref in enumerate((sda_ref, sg_ref)):
                    carry2_ref[2 * q + j] = jnp.broadcast_to(s_ref[q, 0:1, :], carry2_ref.shape[1:])

    P = p_ref[...]
    mu2 = mu2_ref[...]
    mus = (P[0:1], P[1:2], P[2:3], mu2[0:1], mu2[1:2])
    x_refs = (r_ref, k_ref, v_ref, da_ref, g_ref)
    s_refs = (sr_ref, sk_ref, sv_ref, sda_ref, sg_ref)
    us_all = []
    for q in range(n_long):
        us = []
        for j in range(5):
            x = x_refs[j][...] if stacked else x_refs[j][q]
            rolled = pltpu.roll(x, 1, axis=0)
            if stacked:
                prev = jnp.where(seq_start, s_refs[j][...], rolled)
            else:
                cref, jj = (carry_ref, 3 * q + j) if j < 3 else (carry2_ref, 2 * q + j - 3)
                prev = jnp.where(rcol == 0, cref[jj][SUB - 1:SUB, :], rolled)
                prev = jnp.where(rglob == first_valid, s_refs[j][q, 0:1, :], prev)
                cref[jj] = x[CHUNK - SUB:CHUNK, :]
            us.append(x + (prev - x) * mus[j])
        us_all.append(us)

    cat = lambda xs: xs[0] if len(xs) == 1 else jnp.concatenate(xs, axis=0)
    u_da = cat([us[3] for us in us_all])
    u_g = cat([us[4] for us in us_all])
    valid_all = cat([valid] * n_long)
    xw = P[3:4] + _mm(jnp.tanh(u_da), wd_ref[...])
    w_log = -_softplus(-xw) - 0.5
    lw = jnp.where(valid_all, -jnp.exp(w_log), 0.0)
    a = jax.nn.sigmoid(P[4:5] + _mm(u_da, wa_ref[...]))
    g = _mm(jax.nn.sigmoid(u_g), wg_ref[...])

    gens = []
    for q in range(n_long):
        rs = slice(q * CHUNK, (q + 1) * CHUNK)
        r, k, v = us_all[q][:3]
        for pr in range(n_pairs):
            sl = slice(pr * LANE, (pr + 1) * LANE)
            prm = tuple(P[i:i + 1, sl] for i in range(5, 10))
            if stacked:
                states = [hb0_ref[s, 0] for s in range(CHUNK >> seq_log2)]
            else:
                states = [hb_ref[q * n_pairs + pr]]
            gens.append(_rwkv_pair_steps(r[:, sl], k[:, sl], v[:, sl], lw[rs, sl], a[rs, sl], g[rs, sl], prm,
                                         valid, states, seq_log2))
    for idx, (out, new_states) in enumerate(_run_interleaved(gens)):
        q, pr = divmod(idx, n_pairs)
        sl = slice(pr * LANE, (pr + 1) * LANE)
        if stacked:
            o_ref[:, sl] = out
            for s, st in enumerate(new_states):
                hb_out_ref[s, 0] = st
        else:
            o_ref[q, :, sl] = out
            hb_ref[idx] = new_states[0]

    if not stacked:
        @pl.when(c == n_c - 1)
        def _():
            for q in range(n_long):
                hb_out_ref[q] = hb_ref[q * n_pairs:(q + 1) * n_pairs]


def rwkv(proj, shift_rows, hb0, params, mu2, wd, wa, wg, seq_len, first_valid):
    n, _ = proj.shape
    n_pr = H_B // 2
    stacked = seq_len < CHUNK
    if stacked:
        assert CHUNK % seq_len == 0 and n % CHUNK == 0 and seq_len & (seq_len - 1) == 0
        seq_log2 = seq_len.bit_length() - 1
        n_pairs, per_step = 1, CHUNK // seq_len
        grid = (n // CHUNK, n_pr)
        rows = lambda g, pr: g
        pcol = lambda g, pr: pr
        hb_block = (per_step, 1, CHUNK, CHUNK)
        hb_map = lambda g, pr: (g, pr, 0, 0)
        s_rows = CHUNK
        srow = lambda g, pr: (g,)
        scratch = []
    else:
        assert seq_len % CHUNK == 0 and n % seq_len == 0
        seq_log2 = 7
        n_c = seq_len // CHUNK
        n_pairs = n_pr
        n_seqs = n // seq_len
        n_long = next(d for d in (4, 2, 1) if n_seqs % d == 0)
        proj = proj.reshape(n_seqs, seq_len, proj.shape[1])
        grid = (n_seqs // n_long, n_c)
        pcol = lambda bi, c: 0
        hb_block = (n_long, n_pr, CHUNK, CHUNK)
        hb_map = lambda bi, c: (bi, 0, 0, 0)
        scratch = [pltpu.VMEM((n_long * n_pr, CHUNK, CHUNK), F32), pltpu.VMEM((3 * n_long, SUB, W_B), F32),
                   pltpu.VMEM((2 * n_long, SUB, LANE), F32)]
    w = n_pairs * LANE
    q0 = 3 * W_A

    def x_spec(col0, width, per_pair):
        blk0 = col0 // width
        if stacked:
            return pl.BlockSpec((CHUNK, width), lambda i, j: (rows(i, j), blk0 + (pcol(i, j) if per_pair else 0)))
        return pl.BlockSpec((n_long, CHUNK, width), lambda i, j: (i, j, blk0))

    def s_spec(col0, width, per_pair):
        blk0 = col0 // width
        if stacked:
            return pl.BlockSpec((s_rows, width), lambda i, j: srow(i, j) + (blk0 + (pcol(i, j) if per_pair else 0),))
        return pl.BlockSpec((n_long, SUB, width), lambda i, j: (i, 0, blk0))

    cols = [(0, w, True), (W_B, w, True), (2 * W_B, w, True), (3 * W_B, LANE, False), (3 * W_B + LANE, LANE, False)]
    in_specs = ([x_spec(q0 + c0, wd_, pp) for c0, wd_, pp in cols]
                + [s_spec(c0, wd_, pp) for c0, wd_, pp in cols]
                + [pl.BlockSpec(hb_block, hb_map),
                   pl.BlockSpec((16, w), lambda i, j: (0, pcol(i, j))),
                   pl.BlockSpec((SUB, LANE), lambda i, j: (0, 0)),
                   pl.BlockSpec((LANE, w), lambda i, j: (0, pcol(i, j))),
                   pl.BlockSpec((LANE, w), lambda i, j: (0, pcol(i, j))),
                   pl.BlockSpec((LANE, w), lambda i, j: (0, pcol(i, j)))])
    kern = functools.partial(_rwkv_kernel, n_pairs=n_pairs, seq_log2=seq_log2, first_valid=first_valid)
    if stacked:
        o_spec = pl.BlockSpec((CHUNK, w), lambda i, j: (rows(i, j), pcol(i, j)))
        o_shape = jax.ShapeDtypeStruct((n, W_B), F32)
    else:
        o_spec = pl.BlockSpec((n_long, CHUNK, w), lambda i, j: (i, j, 0))
        o_shape = jax.ShapeDtypeStruct((n_seqs, seq_len, W_B), F32)
    out, hb = pl.pallas_call(
        kern,
        grid=grid,
        in_specs=in_specs,
        out_specs=[o_spec, pl.BlockSpec(hb_block, hb_map)],
        out_shape=[o_shape, jax.ShapeDtypeStruct(hb0.shape, F32)],
        scratch_shapes=scratch,
        compiler_params=_cparams(("parallel", "arbitrary")),
        name="rwkv",
    )(proj, proj, proj, proj, proj, shift_rows, shift_rows, shift_rows, shift_rows, shift_rows,
      hb0, params, mu2, wd, wa, wg)
    return out.reshape(n, W_B), hb


def _scan_rows(x, n):
    ridx = lax.broadcasted_iota(jnp.int32, (n, 1), 0)
    s = 1
    while s < n:
        x = x + jnp.where(ridx >= s, pltpu.roll(x, s, axis=0), 0.0)
        s *= 2
    return x


def _scan_lanes(x):
    lidx = lax.broadcasted_iota(jnp.int32, (1, LANE), 1)
    s = 1
    while s < LANE:
        x = x + jnp.where(lidx >= s, pltpu.roll(x, s, axis=1), 0.0)
        s *= 2
    return x


def _mlstm_kernel(q_ref, k_ref, v_ref, og_ref, gc_ref, gt_ref, conv0_ref, c0_ref, n0_ref, m0_ref,
                  cw_ref, cb_ref, gb_row_ref, gb_col_ref, nw_ref,
                  o_ref, c_out_ref, n_out_ref, m_out_ref,
                  cext_ref, m_ref, halo_ref, *, first_valid):
    c = pl.program_id(1)
    n_c = pl.num_programs(1)
    n_seq, lq = q_ref.shape[0], q_ref.shape[1]
    rcol = lax.broadcasted_iota(jnp.int32, (lq, 1), 0)
    rglob = c * lq + rcol
    valid_c = rglob >= first_valid
    lrow = lax.broadcasted_iota(jnp.int32, (1, LANE), 1)
    valid_r = ((c * lq + lrow) >= first_valid) & (lrow < lq)
    causal = lax.broadcasted_iota(jnp.int32, (lq, LANE), 1) <= lax.broadcasted_iota(jnp.int32, (lq, LANE), 0)
    lane_is_f = (lrow >= H_C) & (lrow < 2 * H_C)
    row_is_f = lax.broadcasted_iota(jnp.int32, (SUB, 1), 0) >= H_C
    ones_ext = jnp.ones((lq, LANE), F32)
    cw = cw_ref[...]

    @pl.when(c == 0)
    def _():
        for sq in range(n_seq):
            for h in range(H_C):
                cext_ref[sq * H_C + h, :, :DV_C] = c0_ref[0, sq, h]
                cext_ref[sq * H_C + h, :, DV_C:] = jnp.broadcast_to(n0_ref[sq, h, 0:1, :], (DK_C, LANE)).T
            m_ref[sq * H_C:(sq + 1) * H_C] = m0_ref[sq]
            halo_ref[sq] = conv0_ref[sq]

    def seq_terms(sq):
        x = jnp.where(valid_c, jnp.concatenate([q_ref[sq], k_ref[sq]], axis=1), 0.0)
        xcat = jnp.concatenate([halo_ref[sq], x], axis=0)
        conv = cb_ref[...] + x * cw[CONV_W - 1:CONV_W]
        for dlt in range(1, CONV_W):
            conv = conv + pltpu.roll(xcat, dlt, axis=0)[SUB:SUB + lq] * cw[CONV_W - 1 - dlt:CONV_W - dlt]
        halo_ref[sq] = xcat[lq:lq + SUB]
        qk = conv * jax.nn.sigmoid(conv)

        gcol = gc_ref[sq] + gb_row_ref[...]
        gcol = jnp.where(lane_is_f, -_softplus(-gcol), gcol)
        gcol = jnp.where(valid_c, gcol, jnp.where(lane_is_f, 0.0, -jnp.inf))
        bcum_cols = _scan_rows(jnp.where(lane_is_f, gcol, 0.0), lq)
        grow = gt_ref[sq] + gb_col_ref[...]
        grow = jnp.where(row_is_f, -_softplus(-grow), grow)
        grow = jnp.where(valid_r, grow, jnp.where(row_is_f, 0.0, -jnp.inf))
        bcum_rows = _scan_lanes(jnp.where(row_is_f, grow, 0.0))
        return qk, gcol, bcum_cols, grow, bcum_rows

    def head_steps(sq, h, qk, gcol, bcum_cols, grow, bcum_rows):
        slot = sq * H_C + h
        q = qk[:, h * DK_C:(h + 1) * DK_C]
        k = qk[:, QK_C + h * DK_C:QK_C + (h + 1) * DK_C] * (DK_C ** -0.5)
        v_ext = jnp.concatenate([v_ref[sq, :, h * DV_C:(h + 1) * DV_C], ones_ext], axis=1)
        li_c = gcol[:, h:h + 1]
        bc_c = bcum_cols[:, H_C + h:H_C + h + 1]
        li_r = grow[h:h + 1, :]
        bc_r = bcum_rows[H_C + h:H_C + h + 1, :]
        m_prev = m_ref[slot, 0:1, 0:1]
        cext = cext_ref[slot]

        dmat = jnp.where(causal, bc_c - bc_r + li_r, -jnp.inf)
        inter = bc_c + m_prev
        m_row = jnp.maximum(inter, jnp.max(dmat, axis=1, keepdims=True))
        k_pad = _pad_rows(k, CHUNK)
        s = _mm(q, k_pad, NT)
        q_c = _mm(q, cext)
        yield
        w_intra = jnp.exp(dmat - m_row)
        w_inter = jnp.exp(inter - m_row)
        s = s * w_intra
        yield
        num_ext = _mm(s, _pad_rows(v_ext, CHUNK)) + w_inter * q_c
        den = num_ext[:, DV_C:DV_C + 1]
        hh = num_ext[:, :DV_C] / jnp.maximum(jnp.abs(den), jnp.exp(-m_row))

        m_new = m_row[lq - 1:lq, :]
        bc_last = bc_c[lq - 1:lq, :]
        w_end = jnp.exp(bc_last - bc_c + li_c - m_new)
        dec = jnp.exp(bc_last + m_prev - m_new)
        cext_ref[slot] = dec * cext + _mm(k_pad, _pad_rows(w_end * v_ext, CHUNK), TN)
        m_ref[slot] = jnp.broadcast_to(m_new, (SUB, LANE))
        yield

        mu = jnp.mean(hh, axis=1, keepdims=True)
        hc = hh - mu
        yield
        var = jnp.mean(hc * hc, axis=1, keepdims=True)
        hn = hc * lax.rsqrt(var + RMS_EPS) * nw_ref[:, h * DV_C:(h + 1) * DV_C]
        o_ref[sq, :, h * DV_C:(h + 1) * DV_C] = hn * jax.nn.sigmoid(og_ref[sq, :, h * DV_C:(h + 1) * DV_C])

    terms = [seq_terms(sq) for sq in range(n_seq)]
    _run_interleaved([head_steps(sq, h, *terms[sq]) for sq in range(n_seq) for h in range(H_C)])

    @pl.when(c == n_c - 1)
    def _():
        for sq in range(n_seq):
            for h in range(H_C):
                c_out_ref[sq, h] = cext_ref[sq * H_C + h, :, :DV_C]
                n_out_ref[sq, h] = cext_ref[sq * H_C + h, :, DV_C:].T[0:SUB, :]
            m_out_ref[sq] = m_ref[sq * H_C:(sq + 1) * H_C]


def mlstm(proj, gates_t, conv0, c0_all, layer, n0b, m0b, conv_w, conv_b, gb_row, gb_col, norm_w, first_valid):
    bsz, t, _ = proj.shape
    lq = min(t, CHUNK)
    n_c = t // lq
    assert t % lq == 0 and gates_t.shape[2] == n_c * CHUNK
    qk2 = 2 * QK_C
    nq = next(d for d in (4, 2, 1) if bsz % d == 0)
    kern = functools.partial(_mlstm_kernel, first_valid=first_valid)
    const2 = lambda bi, c: (0, 0)
    return pl.pallas_call(
        kern,
        grid=(bsz // nq, n_c),
        in_specs=[pl.BlockSpec((nq, lq, QK_C), lambda bi, c: (bi, c, 0)),
                  pl.BlockSpec((nq, lq, QK_C), lambda bi, c: (bi, c, 1)),
                  pl.BlockSpec((nq, lq, V_C), lambda bi, c: (bi, c, qk2 // V_C)),
                  pl.BlockSpec((nq, lq, V_C), lambda bi, c: (bi, c, qk2 // V_C + 1)),
                  pl.BlockSpec((nq, lq, LANE), lambda bi, c: (bi, c, (qk2 + 2 * V_C) // LANE)),
                  pl.BlockSpec((nq, SUB, LANE), lambda bi, c: (bi, 0, c)),
                  pl.BlockSpec((nq, SUB, qk2), lambda bi, c: (bi, 0, 0)),
                  pl.BlockSpec((1, nq, H_C, DK_C, DV_C), lambda bi, c: (layer, bi, 0, 0, 0)),
                  pl.BlockSpec((nq, H_C, SUB, DK_C), lambda bi, c: (bi, 0, 0, 0)),
                  pl.BlockSpec((nq, H_C, SUB, LANE), lambda bi, c: (bi, 0, 0, 0)),
                  pl.BlockSpec((SUB, qk2), const2),
                  pl.BlockSpec((1, qk2), const2),
                  pl.BlockSpec((1, LANE), const2),
                  pl.BlockSpec((SUB, LANE), const2),
                  pl.BlockSpec((1, V_C), const2)],
        out_specs=[pl.BlockSpec((nq, lq, V_C), lambda bi, c: (bi, c, 0)),
                   pl.BlockSpec((nq, H_C, DK_C, DV_C), lambda bi, c: (bi, 0, 0, 0)),
                   pl.BlockSpec((nq, H_C, SUB, DK_C), lambda bi, c: (bi, 0, 0, 0)),
                   pl.BlockSpec((nq, H_C, SUB, LANE), lambda bi, c: (bi, 0, 0, 0))],
        out_shape=[jax.ShapeDtypeStruct((bsz, t, V_C), F32),
                   jax.ShapeDtypeStruct((bsz, H_C, DK_C, DV_C), F32),
                   jax.ShapeDtypeStruct((bsz, H_C, SUB, DK_C), F32),
                   jax.ShapeDtypeStruct((bsz, H_C, SUB, LANE), F32)],
        scratch_shapes=[pltpu.VMEM((nq * H_C, DK_C, DV_C + LANE), F32),
                        pltpu.VMEM((nq * H_C, SUB, LANE), F32),
                        pltpu.VMEM((nq, SUB, qk2), F32)],
        compiler_params=_cparams(("parallel", "arbitrary")),
        name="mlstm",
    )(proj, proj, proj, proj, proj, gates_t, conv0, c0_all, n0b, m0b, conv_w, conv_b, gb_row, gb_col, norm_w)


def _pad_to(x, axis, size):
    pad = [(0, 0)] * x.ndim
    pad[axis] = (0, size - x.shape[axis])
    return jnp.pad(x, pad)


def _rwkv_params(i, rwkv_mu, rwkv_w0, rwkv_a0, rwkv_k_k, rwkv_k_a, rwkv_r_k, rwkv_ln_w, rwkv_ln_b,
                 w_decay_up, w_iclr_up, w_gate_up):
    mu = rwkv_mu[i]
    rows = [mu[0:W_B], mu[W_B:2 * W_B], mu[2 * W_B:3 * W_B], rwkv_w0[i], rwkv_a0[i], rwkv_k_k[i],
            rwkv_k_a[i], rwkv_r_k[i], rwkv_ln_w[i], rwkv_ln_b[i]]
    pp = _pad_to(jnp.stack(rows, axis=0), 0, 16)
    mu2 = _pad_to(mu[3 * W_B:].reshape(2, LANE), 0, SUB)
    wd = jnp.concatenate([w_decay_up[i], jnp.zeros((R_ICLR, W_B), F32)], axis=0).astype(BF16)
    wa = jnp.concatenate([jnp.zeros((R_DECAY, W_B), F32), w_iclr_up[i]], axis=0).astype(BF16)
    wg = w_gate_up[i].astype(BF16)
    return pp, mu2, wd, wa, wg


def _pair_states(wkv):
    bsz = wkv.shape[0]
    st = jnp.swapaxes(wkv, -1, -2).reshape(bsz, H_B // 2, 2, DH_B, DH_B)
    z = jnp.zeros_like(st[:, :, 0])
    top = jnp.concatenate([st[:, :, 0], z], axis=-1)
    bot = jnp.concatenate([z, st[:, :, 1]], axis=-1)
    return jnp.concatenate([top, bot], axis=-2)


def _unpair_states(hb):
    bsz = hb.shape[0]
    a = hb[:, :, :DH_B, :DH_B]
    b = hb[:, :, DH_B:, DH_B:]
    st = jnp.stack([a, b], axis=2).reshape(bsz, H_B, DH_B, DH_B)
    return jnp.swapaxes(st, -1, -2)


def kernel(x_prompt, x_sample, cache_k, cache_v, page_table, state_wkv, state_shift, state_mlstm_c, state_mlstm_n, state_mlstm_m, state_conv, meta_tokens, g_mix_pre, g_mix_post, g_ffn_pre, g_ffn_post, w_in_ab, w_out_ab, sb_bias, rwkv_mu, rwkv_w0, rwkv_w_decay_up, rwkv_a0, rwkv_w_iclr_up, rwkv_w_gate_up, rwkv_k_k, rwkv_k_a, rwkv_r_k, rwkv_ln_w, rwkv_ln_b, w_in_c, conv_w, conv_b, mlstm_b_i, mlstm_b_f, mlstm_norm_w, w_out_c, w_ffn_gate, w_ffn_up, w_ffn_down):
    bsz, seq, d = x_prompt.shape
    db, ds, _ = x_sample.shape
    depth = g_mix_pre.shape[0]
    tp = PAD_FRONT + N_META + seq
    assert seq % CHUNK == 0 and ds % SUB == 0 and ds <= CHUNK and seq >= CONV_W and ds >= CONV_W

    meta = jnp.broadcast_to(meta_tokens.astype(F32)[None], (bsz, N_META, d))
    xp = jnp.concatenate([jnp.zeros((bsz, PAD_FRONT, d), F32), meta, x_prompt], axis=1).reshape(bsz * tp, d)
    xs = x_sample.reshape(db * ds, d)

    n_layers_ab, n_pool = cache_k.shape[:2]
    pool_k = jnp.transpose(cache_k, (0, 1, 3, 4, 2)).reshape(n_layers_ab, n_pool, W_A, CHUNK)
    pool_v = jnp.transpose(cache_v, (0, 1, 3, 4, 2)).reshape(n_layers_ab, n_pool, W_A, CHUNK)
    outs = {name: [] for name in ("k_p", "v_p", "wkv_p", "sh_p", "c_p", "n_p", "m_p", "cv_p",
                                  "k_s", "v_s", "wkv_s", "sh_s", "c_s", "n_s", "m_s", "cv_s")}
    row = lambda v: v.reshape(1, -1)
    w_in_ab_b, w_out_ab_b, w_out_c_b = w_in_ab.astype(BF16), w_out_ab.astype(BF16), w_out_c.astype(BF16)
    w_gate_b, w_up_b, w_down_b = w_ffn_gate.astype(BF16), w_ffn_up.astype(BF16), w_ffn_down.astype(BF16)

    for layer in range(depth):
        i = layer // 2
        ffn_args = (row(g_ffn_pre[layer]), w_gate_b, w_up_b, w_down_b, layer, row(g_ffn_post[layer]))
        if layer % 2 == 0:
            pp, mu2, wd, wa, wg = _rwkv_params(i, rwkv_mu, rwkv_w0, rwkv_a0, rwkv_k_k, rwkv_k_a, rwkv_r_k,
                                               rwkv_ln_w, rwkv_ln_b, rwkv_w_decay_up, rwkv_w_iclr_up, rwkv_w_gate_up)
            new_x = []
            for grp, x in (("p", xp), ("s", xs)):
                nb, t = (bsz, tp) if grp == "p" else (db, ds)
                proj = norm_matmul(x, row(g_mix_pre[layer]), w_in_ab_b, i).reshape(nb, t, AB_COLS)
                if grp == "p":
                    o_a = sb_prompt(proj, sb_bias[i], PAD_FRONT)
                    shift_rows = jnp.zeros((nb, SUB, B_COLS), F32)
                    hb0 = jnp.zeros((nb, H_B // 2, CHUNK, CHUNK), F32)
                    first_valid = PAD_FRONT
                else:
                    o_a = sb_sample(proj, pool_k, pool_v, i, page_table, sb_bias[i])
                    shift_rows = _pad_to(state_shift[i][:, None, :], 1, t).reshape(nb * t, B_COLS)
                    hb0 = _pair_states(state_wkv[i])
                    first_valid = 0
                o_b, hb = rwkv(proj.reshape(nb * t, AB_COLS), shift_rows, hb0, pp, mu2, wd, wa, wg, t, first_valid)
                new_x.append(mix_ffn([o_a.reshape(nb * t, W_A), o_b.reshape(nb * t, W_B)], w_out_ab_b, i,
                                     row(g_mix_post[layer]), x, *ffn_args))
                outs["k_" + grp].append(proj[:, first_valid:, W_A:2 * W_A].reshape(nb, t - first_valid, H_A, DH_A))
                outs["v_" + grp].append(proj[:, first_valid:, 2 * W_A:3 * W_A].reshape(nb, t - first_valid, H_A, DH_A))
                outs["wkv_" + grp].append(_unpair_states(hb))
                outs["sh_" + grp].append(proj[:, t - 1, 3 * W_A:])
            xp, xs = new_x
        else:
            wc = w_in_c[i]
            ncol = wc.shape[1]
            wperm = jnp.concatenate([wc[:, :2 * QK_C + V_C], wc[:, ncol - V_C:],
                                     wc[:, 2 * QK_C + V_C:ncol - V_C]], axis=1)
            wperm = _pad_to(wperm, 1, C_COLS_PAD).astype(BF16)[None]
            cw = _pad_to(conv_w[i], 0, SUB)
            gb = jnp.concatenate([mlstm_b_i[i], mlstm_b_f[i]])
            gb_row = _pad_to(gb, 0, LANE).reshape(1, LANE)
            gb_col = jnp.broadcast_to(gb[:, None], (SUB, LANE))
            new_x = []
            for grp, x in (("p", xp), ("s", xs)):
                nb, t = (bsz, tp) if grp == "p" else (db, ds)
                proj = norm_matmul(x, row(g_mix_pre[layer]), wperm, 0).reshape(nb, t, C_COLS_PAD)
                g0 = 2 * QK_C + 2 * V_C
                gates_t = jnp.swapaxes(proj[:, :, g0:g0 + SUB], 1, 2)
                if grp == "p":
                    first_valid = PAD_FRONT
                    conv0 = jnp.zeros((nb, SUB, 2 * QK_C), F32)
                    c0_all, c0_layer = jnp.zeros((1, nb, H_C, DK_C, DV_C), F32), 0
                    n0 = jnp.zeros((nb, H_C, DK_C), F32)
                    m0 = jnp.zeros((nb, H_C), F32)
                else:
                    first_valid = 0
                    gates_t = _pad_to(gates_t, 2, CHUNK)
                    conv0 = jnp.concatenate([jnp.zeros((nb, SUB - (CONV_W - 1), 2 * QK_C), F32), state_conv[i]], axis=1)
                    c0_all, c0_layer = state_mlstm_c, i
                    n0, m0 = state_mlstm_n[i], state_mlstm_m[i]
                n0b = jnp.broadcast_to(n0[:, :, None, :], (nb, H_C, SUB, DK_C))
                m0b = jnp.broadcast_to(m0[..., None, None], (nb, H_C, SUB, LANE))
                hmix, c_new, n_new, m_new = mlstm(proj, gates_t, conv0, c0_all, c0_layer, n0b, m0b, cw,
                                                  row(conv_b[i]), gb_row, gb_col, row(mlstm_norm_w[i]), first_valid)
                new_x.append(mix_ffn([hmix.reshape(nb * t, V_C)], w_out_c_b, i, row(g_mix_post[layer]), x,
                                     *ffn_args))
                outs["c_" + grp].append(c_new)
                outs["n_" + grp].append(n_new[:, :, 0, :])
                outs["m_" + grp].append(m_new[..., 0, 0])
                outs["cv_" + grp].append(proj[:, t - (CONV_W - 1):, :2 * QK_C])
            xp, xs = new_x

    y_prompt = xp.reshape(bsz, tp, d)[:, PAD_FRONT + N_META:]
    y_sample = xs.reshape(db, ds, d)
    st = lambda name: jnp.stack(outs[name])
    return (y_prompt, y_sample, st("k_p"), st("v_p"), st("wkv_p"), st("sh_p"), st("c_p"), st("n_p"),
            st("m_p"), st("cv_p"), st("k_s"), st("v_s"), st("wkv_s"), st("sh_s"), st("c_s"), st("n_s"),
            st("m_s"), st("cv_s"))
```
